```python
import math
import jax, jax.numpy as jnp
from jax import lax
import numpy as np

D_MODEL = 2048
BATCH = 2
SEQ = 4096
DEPTH = 2
DEC_BATCH = 128
DEC_SEQ = 8
PAST_LEN = 16384
PAGE_SIZE = 128

MLA_HEADS = 4
MLA_Q_RANK = 512
MLA_KV_RANK = 256
MLA_NOPE_DIM = 128
MLA_ROPE_DIM = 64
MLA_V_DIM = 128
MLA_SCALE = (MLA_NOPE_DIM + MLA_ROPE_DIM) ** -0.5
MOBA_HEADS = 4
MOBA_KV_HEADS = 1
MOBA_HEAD_DIM = 128
MOBA_BLOCK = 256
MOBA_TOPK = 3
MOBA_SCALE = MOBA_HEAD_DIM ** -0.5
MOBA_Q_CHUNK = 64
SSM_HEADS = 16
SSM_HEAD_DIM = 64
SSM_D_INNER = SSM_HEADS * SSM_HEAD_DIM
SSM_GROUPS = 2
SSM_STATE = 128
SSM_CHUNK = 128
CONV_WIDTH = 4
CONV_DIM = SSM_D_INNER + 2 * SSM_GROUPS * SSM_STATE
MIX_WIDTH = MLA_HEADS * MLA_V_DIM + MOBA_HEADS * MOBA_HEAD_DIM + SSM_D_INNER
IN_SPLIT_SIZES = (MLA_Q_RANK, MLA_KV_RANK, MLA_ROPE_DIM,
                  MOBA_HEADS * MOBA_HEAD_DIM, MOBA_KV_HEADS * MOBA_HEAD_DIM, MOBA_KV_HEADS * MOBA_HEAD_DIM,
                  SSM_D_INNER, CONV_DIM, SSM_HEADS)
IN_COLS = sum(IN_SPLIT_SIZES)
ATTN_Q_BLOCK = 128
D_FF = 5632
MACARON_WEIGHT = 0.5
ROPE_THETA = 10000.0
ALPHA = (2 * DEPTH) ** 0.25
BETA = (8 * DEPTH) ** -0.25
LN_EPS = 1e-5
RMS_EPS = 1e-6

kernel_name = 'hybrid_mla_moba_ssd_decoder_step'


def split_cols(v, sizes):
    return jnp.split(v, np.cumsum(sizes)[:-1].tolist(), axis=-1)


def layer_norm(v, g, b):
    v32 = v.astype(jnp.float32)
    mu = jnp.mean(v32, -1, keepdims=True)
    var = jnp.mean(jnp.square(v32 - mu), -1, keepdims=True)
    return ((v32 - mu) * lax.rsqrt(var + LN_EPS) * g + b).astype(v.dtype)


def rms_norm(v, g):
    v32 = v.astype(jnp.float32)
    return (v32 * lax.rsqrt(jnp.mean(jnp.square(v32), -1, keepdims=True) + RMS_EPS) * g).astype(v.dtype)


def rope(v, pos):
    half = v.shape[-1] // 2
    inv = ROPE_THETA ** (-jnp.arange(half, dtype=jnp.float32) / half)
    ang = pos.astype(jnp.float32)[:, None] * inv[None, :]
    cos = jnp.cos(ang)[:, None, :]
    sin = jnp.sin(ang)[:, None, :]
    v32 = v.astype(jnp.float32)
    v1, v2 = v32[..., :half], v32[..., half:]
    return jnp.concatenate([v1 * cos - v2 * sin, v2 * cos + v1 * sin], axis=-1).astype(v.dtype)


def swiglu(h, w_gate, w_up, w_down):
    return (jax.nn.silu(h @ w_gate) * (h @ w_up)) @ w_down


def modulate(x, m):
    return x * (1.0 + m[:, 1]) + m[:, 0]


def residual_norm(x, out, m, g, b, weight):
    return layer_norm(ALPHA * x + weight * m[:, 2] * out, g, b)


def mla_attend(q_lat, q_rope, ckv, krope, q_pos):
    s = jnp.einsum('bthr,blr->bhtl', q_lat, ckv, preferred_element_type=jnp.float32)
    s = s + jnp.einsum('bthd,bld->bhtl', q_rope, krope, preferred_element_type=jnp.float32)
    visible = jnp.arange(ckv.shape[1])[None, :] <= q_pos[:, None]
    p = jax.nn.softmax(jnp.where(visible, s * MLA_SCALE, -jnp.inf), axis=-1)
    return jnp.einsum('bhtl,blr->bthr', p.astype(ckv.dtype), ckv)


def moba_blocks(k, v):
    B, L = k.shape[:2]
    nb = -(-L // MOBA_BLOCK)
    pad = nb * MOBA_BLOCK - L

    def blk(a):
        a = jnp.pad(a, ((0, 0), (0, pad), (0, 0), (0, 0)))
        return a.reshape(B, nb, MOBA_BLOCK, MOBA_KV_HEADS, MOBA_HEAD_DIM).transpose(0, 3, 1, 2, 4)

    kb, vb = blk(k), blk(v)
    kmean = jnp.mean(kb.astype(jnp.float32), axis=3)
    return kb, vb, kmean


def moba_attend(q, q_pos, kb, vb, kmean):
    B, T = q.shape[:2]
    nb = kb.shape[2]
    group = MOBA_HEADS // MOBA_KV_HEADS
    qg = q.reshape(B, T, MOBA_KV_HEADS, group, MOBA_HEAD_DIM)
    own = q_pos // MOBA_BLOCK
    gate = jnp.einsum('bthgd,bhnd->bthgn', qg.astype(jnp.float32), kmean)
    fully_past = jnp.arange(nb)[None, :] < own[:, None]
    gate = jnp.where(fully_past[None, :, None, None, :], gate, -jnp.inf)
    n_sel = min(MOBA_TOPK, nb)
    _, sel = lax.top_k(gate, n_sel)
    sel_ok = sel < own[None, :, None, None, None]
    b_i = jnp.arange(B)[:, None, None, None, None]
    h_i = jnp.arange(MOBA_KV_HEADS)[None, None, :, None, None]
    k_sel = kb[b_i, h_i, sel]
    v_sel = vb[b_i, h_i, sel]
    b_o = jnp.arange(B)[:, None, None]
    h_o = jnp.arange(MOBA_KV_HEADS)[None, None, :]
    k_own = kb[b_o, h_o, own[None, :, None]]
    v_own = vb[b_o, h_o, own[None, :, None]]
    s_sel = jnp.einsum('bthgd,bthgskd->bthgsk', qg, k_sel, preferred_element_type=jnp.float32) * MOBA_SCALE
    s_sel = jnp.where(sel_ok[..., None], s_sel, -jnp.inf).reshape(B, T, MOBA_KV_HEADS, group, n_sel * MOBA_BLOCK)
    own_pos = own[:, None] * MOBA_BLOCK + jnp.arange(MOBA_BLOCK)[None, :]
    causal = own_pos <= q_pos[:, None]
    s_own = jnp.einsum('bthgd,bthkd->bthgk', qg, k_own, preferred_element_type=jnp.float32) * MOBA_SCALE
    s_own = jnp.where(causal[None, :, None, None, :], s_own, -jnp.inf)
    p = jax.nn.softmax(jnp.concatenate([s_sel, s_own], axis=-1), axis=-1).astype(vb.dtype)
    p_sel = p[..., :n_sel * MOBA_BLOCK].reshape(B, T, MOBA_KV_HEADS, group, n_sel, MOBA_BLOCK)
    p_own = p[..., n_sel * MOBA_BLOCK:]
    out = (jnp.einsum('bthgsk,bthgskd->bthgd', p_sel, v_sel)
           + jnp.einsum('bthgk,bthkd->bthgd', p_own, v_own))
    return out.reshape(B, T, MOBA_HEADS, MOBA_HEAD_DIM)


def segsum(a):
    n = a.shape[-1]
    cs = jnp.cumsum(a, axis=-1)
    d = cs[..., :, None] - cs[..., None, :]
    return jnp.where(jnp.tril(jnp.ones((n, n), dtype=bool)), d, -jnp.inf)


def ssd(x, dt, a_head, b_in, c_in, init_state):
    bsz, L, H, P = x.shape
    G, N = b_in.shape[2:]
    E = H // G
    lc = min(SSM_CHUNK, L)
    pad = (-L) % lc
    xdt = x.astype(jnp.float32) * dt[..., None]
    adt = dt * a_head
    bf = b_in.astype(jnp.float32)
    cf = c_in.astype(jnp.float32)
    if pad:
        xdt = jnp.pad(xdt, ((0, 0), (0, pad), (0, 0), (0, 0)))
        adt = jnp.pad(adt, ((0, 0), (0, pad), (0, 0)))
        bf = jnp.pad(bf, ((0, 0), (0, pad), (0, 0), (0, 0)))
        cf = jnp.pad(cf, ((0, 0), (0, pad), (0, 0), (0, 0)))
    nc = (L + pad) // lc
    X = xdt.reshape(bsz, nc, lc, G, E, P)
    A = adt.reshape(bsz, nc, lc, G, E).transpose(0, 3, 4, 1, 2)
    Bc = bf.reshape(bsz, nc, lc, G, N)
    Cc = cf.reshape(bsz, nc, lc, G, N)
    a_cs = jnp.cumsum(A, axis=-1)
    decay = jnp.exp(segsum(A))
    cb = jnp.einsum('bclgn,bcsgn->bcgls', Cc, Bc)
    y_diag = jnp.einsum('bcgls,bgecls,bcsgep->bclgep', cb, decay, X)
    decay_states = jnp.exp(a_cs[..., -1:] - a_cs)
    states = jnp.einsum('bclgn,bgecl,bclgep->bcgepn', Bc, decay_states, X)
    init = init_state.astype(jnp.float32).reshape(bsz, G, E, P, N)
    states = jnp.concatenate([init[:, None], states], axis=1)
    chunk_tot = jnp.pad(a_cs[..., -1], ((0, 0), (0, 0), (0, 0), (1, 0)))
    decay_chunk = jnp.exp(segsum(chunk_tot))
    new_states = jnp.einsum('bgezc,bcgepn->bzgepn', decay_chunk, states)
    states_in, final = new_states[:, :-1], new_states[:, -1]
    y_off = jnp.einsum('bclgn,bcgepn,bgecl->bclgep', Cc, states_in, jnp.exp(a_cs))
    y = (y_diag + y_off).reshape(bsz, nc * lc, H, P)[:, :L]
    return y, final.reshape(bsz, H, P, N)


def gated_rms_norm(y, z, g):
    v = y.astype(jnp.float32) * jax.nn.silu(z.astype(jnp.float32))
    bsz, T, _ = v.shape
    vg = v.reshape(bsz, T, SSM_GROUPS, SSM_D_INNER // SSM_GROUPS)
    vg = vg * lax.rsqrt(jnp.mean(jnp.square(vg), -1, keepdims=True) + RMS_EPS)
    return vg.reshape(bsz, T, SSM_D_INNER) * g


def causal_conv(xpad, w, b):
    y = lax.conv_general_dilated(xpad, w.astype(xpad.dtype), (1,), 'VALID',
                                 dimension_numbers=('NWC', 'WIO', 'NWC'),
                                 feature_group_count=xpad.shape[-1])
    return jax.nn.silu(y + b)


def to_blocks(a, size):
    bsz, T = a.shape[:2]
    return a.reshape(bsz, T // size, size, *a.shape[2:]).swapaxes(0, 1)


def from_blocks(a):
    n, bsz, size = a.shape[:3]
    return a.swapaxes(0, 1).reshape(bsz, n * size, *a.shape[3:])


def prompt_attn(q_lat, q_rope, ckv, krope, q_m, k_m, v_m, pos):
    lat = lax.map(lambda a: mla_attend(a[0], a[1], ckv, krope, a[2]),
                  (to_blocks(q_lat, ATTN_Q_BLOCK), to_blocks(q_rope, ATTN_Q_BLOCK),
                   pos.reshape(-1, ATTN_Q_BLOCK)))
    kb, vb, kmean = moba_blocks(k_m, v_m)
    mb = lax.map(lambda a: moba_attend(a[0], a[1], kb, vb, kmean),
                 (to_blocks(q_m, MOBA_Q_CHUNK), pos.reshape(-1, MOBA_Q_CHUNK)))
    return from_blocks(lat), from_blocks(mb)


def paged_attn(ckv_pool, kr_pool, k_pool, v_pool, page_table, q_lat, q_rope, ckv_new, krope_new, q_m, k_m, v_m, pos):
    def one_seq(args):
        ql, qr, cn, kn, qs, ks, vs, pages = args
        ckv = jnp.concatenate([ckv_pool[pages].reshape(-1, MLA_KV_RANK), cn], axis=0)
        kr = jnp.concatenate([kr_pool[pages].reshape(-1, MLA_ROPE_DIM), kn], axis=0)
        lat = mla_attend(ql[None], qr[None], ckv[None], kr[None], pos)[0]
        k_all = jnp.concatenate([k_pool[pages].reshape(-1, MOBA_KV_HEADS, MOBA_HEAD_DIM), ks], axis=0)
        v_all = jnp.concatenate([v_pool[pages].reshape(-1, MOBA_KV_HEADS, MOBA_HEAD_DIM), vs], axis=0)
        kb, vb, kmean = moba_blocks(k_all[None], v_all[None])
        mb = moba_attend(qs[None], pos, kb, vb, kmean)[0]
        return lat, mb
    return lax.map(one_seq, (q_lat, q_rope, ckv_new, krope_new, q_m, k_m, v_m, page_table))


def token_mixer(h, pos, conv_buf, ssm_init, attn, w_in, g_q, w_uq, g_kv, w_uk, w_uv,
                conv_w, conv_b, dt_bias, a_log, d_skip, g_ssm, w_out):
    bsz, T, _ = h.shape
    cq, ckv_raw, kr_raw, q_m, k_m, v_m, z, xbc, dt_raw = split_cols(h @ w_in, IN_SPLIT_SIZES)
    q = (rms_norm(cq, g_q) @ w_uq).reshape(bsz, T, MLA_HEADS, MLA_NOPE_DIM + MLA_ROPE_DIM)
    q_nope = q[..., :MLA_NOPE_DIM]
    q_rope = rope(q[..., MLA_NOPE_DIM:], pos)
    ckv = rms_norm(ckv_raw, g_kv)
    krope = rope(kr_raw[:, :, None, :], pos)[:, :, 0, :]
    q_lat = jnp.einsum('bthn,hrn->bthr', q_nope, w_uk)
    q_m = rope(q_m.reshape(bsz, T, MOBA_HEADS, MOBA_HEAD_DIM), pos)
    k_m = rope(k_m.reshape(bsz, T, MOBA_KV_HEADS, MOBA_HEAD_DIM), pos)
    v_m = v_m.reshape(bsz, T, MOBA_KV_HEADS, MOBA_HEAD_DIM)
    lat_out, moba_out = attn(q_lat, q_rope, ckv, krope, q_m, k_m, v_m, pos)
    mla_out = jnp.einsum('bthr,hrv->bthv', lat_out, w_uv).reshape(bsz, T, MLA_HEADS * MLA_V_DIM)
    xpad = jnp.concatenate([conv_buf.astype(xbc.dtype), xbc], axis=1)
    xs, b_ssm, c_ssm = split_cols(causal_conv(xpad, conv_w, conv_b),
                                  (SSM_D_INNER, SSM_GROUPS * SSM_STATE, SSM_GROUPS * SSM_STATE))
    xs = xs.reshape(bsz, T, SSM_HEADS, SSM_HEAD_DIM)
    dt = jax.nn.softplus((dt_raw + dt_bias).astype(jnp.float32))
    a_head = -jnp.exp(a_log.astype(jnp.float32))
    y, ssm_final = ssd(xs, dt, a_head, b_ssm.reshape(bsz, T, SSM_GROUPS, SSM_STATE),
                       c_ssm.reshape(bsz, T, SSM_GROUPS, SSM_STATE), ssm_init)
    y = y + d_skip.astype(jnp.float32)[:, None] * xs.astype(jnp.float32)
    ssm_out = gated_rms_norm(y.reshape(bsz, T, SSM_D_INNER), z, g_ssm).astype(h.dtype)
    mixed = jnp.concatenate([mla_out, moba_out.reshape(bsz, T, MOBA_HEADS * MOBA_HEAD_DIM), ssm_out], axis=-1) @ w_out
    new_state = (ckv, krope, k_m, v_m, ssm_final, xpad[:, -(CONV_WIDTH - 1):])
    return mixed, new_state


def trunk(x, c, pos, conv_init, ssm_init, attn_for_layer, p):
    bsz = x.shape[0]
    rows = []
    for l in range(DEPTH):
        mod = (jax.nn.silu(c) @ p['w_ada'][l] + p['b_ada'][l]).reshape(bsz, 3, 3, 1, D_MODEL)
        m = mod[:, 0]
        f = swiglu(modulate(x, m), p['ffn_w_gate'][l, 0], p['ffn_w_up'][l, 0], p['ffn_w_down'][l, 0])
        x = residual_norm(x, f, m, p['ln_g'][l, 0], p['ln_b'][l, 0], MACARON_WEIGHT)
        m = mod[:, 1]
        mixed, st = token_mixer(modulate(x, m), pos, conv_init[l], ssm_init[l], attn_for_layer(l),
                                p['w_in'][l], p['g_q'][l], p['w_uq'][l], p['g_kv'][l], p['w_uk'][l], p['w_uv'][l],
                                p['conv_w'][l], p['conv_b'][l], p['dt_bias'][l], p['a_log'][l], p['d_skip'][l],
                                p['g_ssm'][l], p['w_out'][l])
        x = residual_norm(x, mixed, m, p['ln_g'][l, 1], p['ln_b'][l, 1], 1.0)
        m = mod[:, 2]
        f = swiglu(modulate(x, m), p['ffn_w_gate'][l, 1], p['ffn_w_up'][l, 1], p['ffn_w_down'][l, 1])
        x = residual_norm(x, f, m, p['ln_g'][l, 2], p['ln_b'][l, 2], MACARON_WEIGHT)
        rows.append(st)
    return x, [jnp.stack([r[i] for r in rows]) for i in range(len(rows[0]))]


def setup_inputs(seed: int = 0) -> dict:
    key = jax.random.key(seed)
    ks = iter(jax.random.split(key, 40))
    f32 = jnp.float32
    n_pages = PAST_LEN // PAGE_SIZE
    n_used = DEC_BATCH * n_pages
    n_pool = n_used + n_used // 4

    def nrm(shape, scale=1.0):
        return jax.random.normal(next(ks), shape, f32) * scale

    def gain(shape):
        return 1.0 + nrm(shape, 0.02)

    dt0 = jnp.exp(jax.random.uniform(next(ks), (DEPTH, SSM_HEADS), f32, math.log(1e-3), math.log(1e-1)))
    return {
        'x_prompt': nrm((BATCH, SEQ, D_MODEL)),
        'x_sample': nrm((DEC_BATCH, DEC_SEQ, D_MODEL)),
        'cache_mla_ckv': nrm((DEPTH, n_pool, PAGE_SIZE, MLA_KV_RANK)),
        'cache_mla_krope': nrm((DEPTH, n_pool, PAGE_SIZE, MLA_ROPE_DIM)),
        'cache_moba_k': nrm((DEPTH, n_pool, PAGE_SIZE, MOBA_KV_HEADS, MOBA_HEAD_DIM)),
        'cache_moba_v': nrm((DEPTH, n_pool, PAGE_SIZE, MOBA_KV_HEADS, MOBA_HEAD_DIM)),
        'state_ssm': nrm((DEPTH, DEC_BATCH, SSM_HEADS, SSM_HEAD_DIM, SSM_STATE), 0.1),
        'state_conv': nrm((DEPTH, DEC_BATCH, CONV_WIDTH - 1, CONV_DIM)),
        'page_table': jax.random.permutation(next(ks), n_pool)[:n_used].reshape(DEC_BATCH, n_pages).astype(jnp.int32),
        'c_prompt': nrm((BATCH, D_MODEL)),
        'c_sample': nrm((DEC_BATCH, D_MODEL)),
        'w_ada': nrm((DEPTH, D_MODEL, 9 * D_MODEL), D_MODEL ** -0.5),
        'b_ada': nrm((DEPTH, 9 * D_MODEL), 0.02),
        'ln_g': gain((DEPTH, 3, D_MODEL)),
        'ln_b': nrm((DEPTH, 3, D_MODEL), 0.02),
        'ffn_w_gate': nrm((DEPTH, 2, D_MODEL, D_FF), D_MODEL ** -0.5),
        'ffn_w_up': nrm((DEPTH, 2, D_MODEL, D_FF), D_MODEL ** -0.5),
        'ffn_w_down': nrm((DEPTH, 2, D_FF, D_MODEL), BETA * D_FF ** -0.5),
        'w_in': nrm((DEPTH, D_MODEL, IN_COLS), D_MODEL ** -0.5),
        'g_q': gain((DEPTH, MLA_Q_RANK)),
        'w_uq': nrm((DEPTH, MLA_Q_RANK, MLA_HEADS * (MLA_NOPE_DIM + MLA_ROPE_DIM)), MLA_Q_RANK ** -0.5),
        'g_kv': gain((DEPTH, MLA_KV_RANK)),
        'w_uk': nrm((DEPTH, MLA_HEADS, MLA_KV_RANK, MLA_NOPE_DIM), MLA_KV_RANK ** -0.5),
        'w_uv': nrm((DEPTH, MLA_HEADS, MLA_KV_RANK, MLA_V_DIM), MLA_KV_RANK ** -0.5),
        'conv_w': nrm((DEPTH, CONV_WIDTH, 1, CONV_DIM), CONV_WIDTH ** -0.5),
        'conv_b': nrm((DEPTH, CONV_DIM), 0.02),
        'dt_bias': dt0 + jnp.log(-jnp.expm1(-dt0)),
        'a_log': jnp.log(jax.random.uniform(next(ks), (DEPTH, SSM_HEADS), f32, 1.0, 16.0)),
        'd_skip': gain((DEPTH, SSM_HEADS)),
        'g_ssm': gain((DEPTH, SSM_D_INNER)),
        'w_out': nrm((DEPTH, MIX_WIDTH, D_MODEL), BETA * MIX_WIDTH ** -0.5),
    }


def reference(x_prompt, x_sample, cache_mla_ckv, cache_mla_krope, cache_moba_k, cache_moba_v,
              state_ssm, state_conv, page_table, c_prompt, c_sample,
              w_ada, b_ada, ln_g, ln_b, ffn_w_gate, ffn_w_up, ffn_w_down,
              w_in, g_q, w_uq, g_kv, w_uk, w_uv, conv_w, conv_b, dt_bias, a_log, d_skip, g_ssm, w_out):
    p = dict(w_ada=w_ada, b_ada=b_ada, ln_g=ln_g, ln_b=ln_b, ffn_w_gate=ffn_w_gate, ffn_w_up=ffn_w_up,
             ffn_w_down=ffn_w_down, w_in=w_in, g_q=g_q, w_uq=w_uq, g_kv=g_kv, w_uk=w_uk, w_uv=w_uv,
             conv_w=conv_w, conv_b=conv_b, dt_bias=dt_bias, a_log=a_log, d_skip=d_skip, g_ssm=g_ssm, w_out=w_out)
    n_prompt, n_seq = x_prompt.shape[:2]
    pos_p = jnp.arange(n_seq, dtype=jnp.int32)
    conv0 = jnp.zeros((DEPTH, n_prompt, CONV_WIDTH - 1, CONV_DIM), x_prompt.dtype)
    ssm0 = jnp.zeros((DEPTH, n_prompt, SSM_HEADS, SSM_HEAD_DIM, SSM_STATE), jnp.float32)
    y_prompt, prompt_rows = trunk(x_prompt, c_prompt, pos_p, conv0, ssm0, lambda l: prompt_attn, p)
    pos_s = PAST_LEN + jnp.arange(x_sample.shape[1], dtype=jnp.int32)

    def sample_attn_for_layer(l):
        return lambda *a: paged_attn(cache_mla_ckv[l], cache_mla_krope[l], cache_moba_k[l], cache_moba_v[l],
                                     page_table, *a)

    y_sample, sample_rows = trunk(x_sample, c_sample, pos_s, state_conv, state_ssm, sample_attn_for_layer, p)
    ckv_p, krope_p, moba_k_p, moba_v_p, ssm_p, conv_p = prompt_rows
    ckv_s, krope_s, moba_k_s, moba_v_s, ssm_s, conv_s = sample_rows
    return (y_prompt, y_sample, ckv_p, krope_p, moba_k_p, moba_v_p, ssm_p, conv_p,
            ckv_s, krope_s, moba_k_s, moba_v_s, ssm_s, conv_s)
```

```python
import functools
import math

import jax
import jax.numpy as jnp
from jax import lax
from jax.experimental import pallas as pl
from jax.experimental.pallas import tpu as pltpu

F32 = jnp.float32
BF16 = jnp.bfloat16

D_MODEL = 2048
PAGE_SIZE = 128
MLA_HEADS = 4
MLA_Q_RANK = 512
MLA_KV_RANK = 256
MLA_NOPE_DIM = 128
MLA_ROPE_DIM = 64
MLA_V_DIM = 128
MLA_SCALE = (MLA_NOPE_DIM + MLA_ROPE_DIM) ** -0.5
MOBA_HEADS = 4
MOBA_HEAD_DIM = 128
MOBA_BLOCK = 256
MOBA_TOPK = 3
MOBA_SCALE = MOBA_HEAD_DIM ** -0.5
SSM_HEADS = 16
SSM_HEAD_DIM = 64
SSM_D_INNER = SSM_HEADS * SSM_HEAD_DIM
SSM_GROUPS = 2
SSM_STATE = 128
SSM_CHUNK = 128
CONV_WIDTH = 4
CONV_DIM = SSM_D_INNER + 2 * SSM_GROUPS * SSM_STATE
MACARON_WEIGHT = 0.5
ROPE_THETA = 10000.0
LN_EPS = 1e-5
RMS_EPS = 1e-6

COL_CQ = 0
COL_QM = 512
COL_CKV = 1024
COL_KM = 1280
COL_VM = 1408
COL_XBC = 1536
COL_Z = 3072
COL_KRDT = 4096
IN_COLS_PADDED = 4224
DT_LANE = 64

MASK_VALUE = -1e30
PAGES_PER_STEP = 16
VMEM_LIMIT = 56 * 1024 * 1024

HIGHEST = lax.Precision.HIGHEST
NT_DIMS = (((1,), (1,)), ((), ()))
TN_DIMS = (((0,), (0,)), ((), ()))


def _params(*sem):
    return pltpu.CompilerParams(dimension_semantics=sem, vmem_limit_bytes=VMEM_LIMIT)


def _silu(v):
    return v * jax.nn.sigmoid(v)


def _dot(a, b):
    return jnp.dot(a, b, preferred_element_type=F32)


def _dot_nt(a, b, precision=None):
    return lax.dot_general(a, b, NT_DIMS, preferred_element_type=F32, precision=precision)


def _layer_norm_rows(v, g, b):
    mu = jnp.mean(v, axis=-1, keepdims=True)
    d = v - mu
    var = jnp.mean(d * d, axis=-1, keepdims=True)
    return d * lax.rsqrt(var + LN_EPS) * g + b


def _ada_kernel(c_ref, w_ref, b_ref, o_ref):
    h = _silu(c_ref[...]).astype(BF16)
    o_ref[...] = _dot(h, w_ref[...].astype(BF16)) + b_ref[...]


def _ada(c_all, w_ada, b_ada):
    depth, d, n = w_ada.shape
    rows = c_all.shape[0]
    tn = 1024
    return pl.pallas_call(
        _ada_kernel,
        grid=(depth, n // tn),
        in_specs=[
            pl.BlockSpec((rows, d), lambda l, j: (0, 0)),
            pl.BlockSpec((None, d, tn), lambda l, j: (l, 0, j)),
            pl.BlockSpec((None, 1, tn), lambda l, j: (l, 0, j)),
        ],
        out_specs=pl.BlockSpec((None, rows, tn), lambda l, j: (l, 0, j)),
        out_shape=jax.ShapeDtypeStruct((depth, rows, n), F32),
        compiler_params=_params("parallel", "parallel"),
    )(c_all, w_ada, b_ada.reshape(depth, 1, n))


class _Mod:
    def __init__(self, arrays, per_token, tiles_per_seq):
        self.arrays = arrays
        self.per_token = per_token
        self.tiles_per_seq = tiles_per_seq

    def spec(self, tm):
        if self.per_token:
            return pl.BlockSpec((tm, D_MODEL), lambda i, *_: (i, 0))
        tps = self.tiles_per_seq(tm)
        return pl.BlockSpec((None, 1, D_MODEL), lambda i, *_: (i // tps, 0, 0))


def _make_mods(mod, seq_len, per_token):
    n_seq = mod.shape[0]
    m = mod.reshape(n_seq, 3, 3, D_MODEL)
    out = []
    for j in range(3):
        arrays = []
        for k in range(3):
            a = m[:, j, k]
            if per_token:
                a = jnp.broadcast_to(a[:, None, :], (n_seq, seq_len, D_MODEL)).reshape(n_seq * seq_len, D_MODEL)
            else:
                a = a[:, None, :]
            arrays.append(a)
        out.append(_Mod(arrays, per_token, lambda tm: seq_len // tm))
    return out


def _ffn_kernel(x_ref, sh_ref, sc_ref, gt_ref, wg_ref, wu_ref, wd_ref, lg_ref, lb_ref, o_ref,
                h_scr, acc_scr, *, alpha, weight):
    j = pl.program_id(1)

    @pl.when(j == 0)
    def _():
        h_scr[...] = (x_ref[...] * (1.0 + sc_ref[...]) + sh_ref[...]).astype(BF16)
        acc_scr[...] = jnp.zeros_like(acc_scr)

    h = h_scr[...]
    a = _dot(h, wg_ref[...])
    u = _dot(h, wu_ref[...])
    act = (_silu(a) * u).astype(BF16)
    acc_scr[...] += _dot(act, wd_ref[...])

    @pl.when(j == pl.num_programs(1) - 1)
    def _():
        v = alpha * x_ref[...] + weight * gt_ref[...] * acc_scr[...]
        o_ref[...] = _layer_norm_rows(v, lg_ref[...], lb_ref[...])


def _ffn(x, mod, wg, wu, wd, ln_g, ln_b, layer, slot, alpha, tm):
    rows = x.shape[0]
    d_ff = wg.shape[-1]
    tf = 512 if d_ff % 512 == 0 else d_ff
    row_spec = pl.BlockSpec((tm, D_MODEL), lambda i, j: (i, 0))
    vec_spec = pl.BlockSpec((1, D_MODEL), lambda i, j: (0, 0))
    return pl.pallas_call(
        functools.partial(_ffn_kernel, alpha=alpha, weight=MACARON_WEIGHT),
        grid=(rows // tm, d_ff // tf),
        in_specs=[row_spec, mod.spec(tm), mod.spec(tm), mod.spec(tm),
                  pl.BlockSpec((None, None, D_MODEL, tf), lambda i, j: (layer, slot, 0, j)),
                  pl.BlockSpec((None, None, D_MODEL, tf), lambda i, j: (layer, slot, 0, j)),
                  pl.BlockSpec((None, None, tf, D_MODEL), lambda i, j: (layer, slot, j, 0)),
                  vec_spec, vec_spec],
        out_specs=row_spec,
        out_shape=jax.ShapeDtypeStruct((rows, D_MODEL), F32),
        scratch_shapes=[pltpu.VMEM((tm, D_MODEL), BF16), pltpu.VMEM((tm, D_MODEL), F32)],
        compiler_params=_params("parallel", "arbitrary"),
    )(x, mod.arrays[0], mod.arrays[1], mod.arrays[2], wg, wu, wd, ln_g, ln_b)


def _inproj_kernel(x_ref, sh_ref, sc_ref, w_ref, o_ref):
    h = (x_ref[...] * (1.0 + sc_ref[...]) + sh_ref[...]).astype(BF16)
    o_ref[...] = _dot(h, w_ref[...])


def _inproj(x, mod, w_in_p, layer, tm):
    rows = x.shape[0]
    tn = IN_COLS_PADDED // 3
    return pl.pallas_call(
        _inproj_kernel,
        grid=(rows // tm, 3),
        in_specs=[pl.BlockSpec((tm, D_MODEL), lambda i, j: (i, 0)), mod.spec(tm), mod.spec(tm),
                  pl.BlockSpec((None, D_MODEL, tn), lambda i, j: (layer, 0, j))],
        out_specs=pl.BlockSpec((tm, tn), lambda i, j: (i, j)),
        out_shape=jax.ShapeDtypeStruct((rows, IN_COLS_PADDED), F32),
        compiler_params=_params("parallel", "arbitrary"),
    )(x, mod.arrays[0], mod.arrays[1], w_in_p)


def _rope64(v, cos, sin):
    lane = lax.broadcasted_iota(jnp.int32, v.shape, 1)
    swap = jnp.where(lane % 64 < 32, pltpu.roll(v, 96, axis=1), pltpu.roll(v, 32, axis=1))
    return v * cos + swap * sin


def _rope128(v, cos, sin):
    return v * cos + pltpu.roll(v, 64, axis=1) * sin


def _rms(v, g):
    return v * lax.rsqrt(jnp.mean(v * v, axis=-1, keepdims=True) + RMS_EPS) * g


def _prep_kernel(cq_ref, qm_ref, ckvr_ref, km_ref, krdt_ref, cosa_ref, sina_ref, cosb_ref, sinb_ref,
                 gq_ref, wuq_ref, gkv_ref, wuk_ref, dtb_ref,
                 qlat_ref, qrope_ref, ckv_ref, krope_ref, qmo_ref, kmo_ref, dtx_ref):
    cosa, sina = cosa_ref[...], sina_ref[...]
    cosb, sinb = cosb_ref[...], sinb_ref[...]
    cqn = _rms(cq_ref[...], gq_ref[...]).astype(BF16)
    q = _dot(cqn, wuq_ref[...])
    nope_w = MLA_HEADS * MLA_NOPE_DIM
    for pair in range(MLA_HEADS // 2):
        r = _rope64(q[:, nope_w + pair * 128: nope_w + (pair + 1) * 128], cosa, sina)
        qrope_ref[2 * pair] = r[:, :MLA_ROPE_DIM]
        qrope_ref[2 * pair + 1] = r[:, MLA_ROPE_DIM:]
    for h in range(MLA_HEADS):
        qn = q[:, h * MLA_NOPE_DIM:(h + 1) * MLA_NOPE_DIM].astype(BF16)
        qlat_ref[h] = _dot(qn, wuk_ref[h])
    ckv_ref[...] = _rms(ckvr_ref[...], gkv_ref[...])
    krdt = krdt_ref[...]
    krope_ref[...] = _rope64(krdt, cosa, sina)[:, :MLA_ROPE_DIM]
    dtx_ref[...] = jax.nn.softplus(krdt + dtb_ref[...])
    qm = qm_ref[...]
    for h in range(MOBA_HEADS):
        qmo_ref[h] = _rope128(qm[:, h * MOBA_HEAD_DIM:(h + 1) * MOBA_HEAD_DIM], cosb, sinb)
    kmo_ref[...] = _rope128(km_ref[...], cosb, sinb)


def _prep(proj, tabs, tab_tiles, g_q, w_uq_p, g_kv, w_uk_t, dt_bias_p, tm):
    rows = proj.shape[0]

    def col(width, start):
        return pl.BlockSpec((tm, width), lambda i: (i, start // width))

    tab_spec = pl.BlockSpec((tm, 128), lambda i: (i % tab_tiles, 0))

    def full(a):
        return pl.BlockSpec(a.shape, lambda i: (0,) * a.ndim)

    heads_spec = lambda w: pl.BlockSpec((MLA_HEADS, tm, w), lambda i: (0, i, 0))
    row_spec = lambda w: pl.BlockSpec((tm, w), lambda i: (i, 0))
    return pl.pallas_call(
        _prep_kernel,
        grid=(rows // tm,),
        in_specs=[col(512, COL_CQ), col(512, COL_QM), col(256, COL_CKV), col(128, COL_KM), col(128, COL_KRDT),
                  tab_spec, tab_spec, tab_spec, tab_spec,
                  full(g_q), full(w_uq_p), full(g_kv), full(w_uk_t), full(dt_bias_p)],
        out_specs=[heads_spec(MLA_KV_RANK), heads_spec(MLA_ROPE_DIM), row_spec(MLA_KV_RANK), row_spec(MLA_ROPE_DIM),
                   heads_spec(MOBA_HEAD_DIM), row_spec(MOBA_HEAD_DIM), row_spec(128)],
        out_shape=[jax.ShapeDtypeStruct((MLA_HEADS, rows, MLA_KV_RANK), F32),
                   jax.ShapeDtypeStruct((MLA_HEADS, rows, MLA_ROPE_DIM), F32),
                   jax.ShapeDtypeStruct((rows, MLA_KV_RANK), F32),
                   jax.ShapeDtypeStruct((rows, MLA_ROPE_DIM), F32),
                   jax.ShapeDtypeStruct((MOBA_HEADS, rows, MOBA_HEAD_DIM), F32),
                   jax.ShapeDtypeStruct((rows, MOBA_HEAD_DIM), F32),
                   jax.ShapeDtypeStruct((rows, 128), F32)],
        compiler_params=_params("parallel"),
    )(proj, proj, proj, proj, proj, *tabs, g_q, w_uq_p, g_kv, w_uk_t, dt_bias_p)


def _rope_tables(pos):
    def tab(half):
        inv = ROPE_THETA ** (-jnp.arange(half, dtype=F32) / half)
        ang = pos.astype(F32)[:, None] * inv[None, :]
        return jnp.cos(ang), jnp.sin(ang)

    c32, s32 = tab(MLA_ROPE_DIM // 2)
    c64, s64 = tab(MOBA_HEAD_DIM // 2)
    cosa = jnp.concatenate([c32, c32, c32, c32], axis=1)
    sina = jnp.concatenate([-s32, s32, -s32, s32], axis=1)
    cosb = jnp.concatenate([c64, c64], axis=1)
    sinb = jnp.concatenate([-s64, s64], axis=1)
    return cosa, sina, cosb, sinb


def _conv_kernel(cur_ref, prev_ref, init_ref, w_ref, b_ref, o_ref, pad_scr, *, tiles_per_seq, tm):
    i = pl.program_id(0)
    at_start = (i % tiles_per_seq) == 0
    pad_scr[0:8, :] = jnp.where(at_start, init_ref[...], prev_ref[...])
    pad_scr[8:8 + tm, :] = cur_ref[...]
    acc = b_ref[...] + w_ref[CONV_WIDTH - 1:CONV_WIDTH, :] * cur_ref[...]
    for k in range(CONV_WIDTH - 1):
        shift = CONV_WIDTH - 1 - k
        acc = acc + w_ref[k:k + 1, :] * pad_scr[8 - shift:8 - shift + tm, :]
    o_ref[...] = _silu(acc)


def _conv(proj, init8, conv_w, conv_b, seq_len, tm):
    rows = proj.shape[0]
    tiles_per_seq = seq_len // tm
    blk = COL_XBC // CONV_DIM
    return pl.pallas_call(
        functools.partial(_conv_kernel, tiles_per_seq=tiles_per_seq, tm=tm),
        grid=(rows // tm,),
        in_specs=[pl.BlockSpec((tm, CONV_DIM), lambda i: (i, blk)),
                  pl.BlockSpec((8, CONV_DIM), lambda i: (jnp.maximum(i * (tm // 8) - 1, 0), blk)),
                  pl.BlockSpec((None, 8, CONV_DIM), lambda i: (i // tiles_per_seq, 0, 0)),
                  pl.BlockSpec((CONV_WIDTH, CONV_DIM), lambda i: (0, 0)),
                  pl.BlockSpec((1, CONV_DIM), lambda i: (0, 0))],
        out_specs=pl.BlockSpec((tm, CONV_DIM), lambda i: (i, 0)),
        out_shape=jax.ShapeDtypeStruct((rows, CONV_DIM), F32),
        scratch_shapes=[pltpu.VMEM((tm + 8, CONV_DIM), F32)],
        compiler_params=_params("arbitrary"),
    )(proj, proj, init8, conv_w, conv_b)


def _ssd_kernel(xc_ref, dtx_ref, z_ref, init_ref, alog_ref, dskip_ref, gssm_ref, y_ref, st_ref, *, lc):
    c = pl.program_id(1)

    @pl.when(c == 0)
    def _():
        st_ref[...] = init_ref[...]

    gw = SSM_D_INNER // SSM_GROUPS
    xs = xc_ref[:, 0:SSM_D_INNER]
    dt = dtx_ref[:, DT_LANE:DT_LANE + SSM_HEADS]
    a_head = -jnp.exp(alog_ref[...])
    adt = dt * a_head
    row = lax.broadcasted_iota(jnp.int32, (lc, lc), 0)
    colm = lax.broadcasted_iota(jnp.int32, (lc, lc), 1)
    tril = row >= colm
    a_cs = jnp.dot(tril.astype(F32), adt, preferred_element_type=F32, precision=HIGHEST)
    eye = (lax.broadcasted_iota(jnp.int32, (SSM_HEADS, SSM_HEADS), 0)
           == lax.broadcasted_iota(jnp.int32, (SSM_HEADS, SSM_HEADS), 1)).astype(F32)
    a_cs_t = _dot_nt(eye, a_cs, precision=HIGHEST)
    a_tot = a_cs[lc - 1:lc, :]
    spread = (lax.broadcasted_iota(jnp.int32, (SSM_HEADS, SSM_D_INNER), 1) // SSM_HEAD_DIM
              == lax.broadcasted_iota(jnp.int32, (SSM_HEADS, SSM_D_INNER), 0)).astype(F32)
    spread_t = (lax.broadcasted_iota(jnp.int32, (SSM_D_INNER, SSM_HEADS), 0) // SSM_HEAD_DIM
                == lax.broadcasted_iota(jnp.int32, (SSM_D_INNER, SSM_HEADS), 1)).astype(F32)

    def expand(v):
        return jnp.dot(v, spread, preferred_element_type=F32, precision=HIGHEST)

    xdt = xs * expand(dt)
    xd_state = (xdt * expand(jnp.exp(a_tot - a_cs))).astype(BF16)
    lane = lax.broadcasted_iota(jnp.int32, (lc, 128), 1)
    y_groups = []
    ds_groups = []
    for g in range(SSM_GROUPS):
        bg = xc_ref[:, SSM_D_INNER + g * SSM_STATE: SSM_D_INNER + (g + 1) * SSM_STATE].astype(BF16)
        cg = xc_ref[:, SSM_D_INNER + (SSM_GROUPS + g) * SSM_STATE:
                    SSM_D_INNER + (SSM_GROUPS + g + 1) * SSM_STATE].astype(BF16)
        cb = _dot_nt(cg, bg)
        s_g = st_ref[g * gw:(g + 1) * gw, :]
        y_off = _dot_nt(cg, s_g.astype(BF16))
        pairs = []
        for pr in range(gw // 128):
            x_pair = xdt[:, g * gw + pr * 128: g * gw + (pr + 1) * 128]
            acc = None
            for sub in range(2):
                h = (g * gw + pr * 128) // SSM_HEAD_DIM + sub
                seg = a_cs[:, h:h + 1] - a_cs_t[h:h + 1, :]
                decay = jnp.exp(jnp.where(tril, seg, -jnp.inf))
                gmat = (cb * decay).astype(BF16)
                x_h = jnp.where((lane // SSM_HEAD_DIM) == sub, x_pair, 0.0).astype(BF16)
                part = _dot(gmat, x_h)
                acc = part if acc is None else acc + part
            pairs.append(acc)
        y_groups.append((jnp.concatenate(pairs, axis=1), y_off))
        ds_groups.append(lax.dot_general(xd_state[:, g * gw:(g + 1) * gw], bg, TN_DIMS,
                                         preferred_element_type=F32))
    y_diag = jnp.concatenate([p[0] for p in y_groups], axis=1)
    y_off = jnp.concatenate([p[1] for p in y_groups], axis=1)
    y = y_diag + y_off * expand(jnp.exp(a_cs)) + expand(dskip_ref[...]) * xs
    tot_col = jnp.exp(a_cs_t[:, lc - 1:lc])
    decay_full = jnp.dot(spread_t, jnp.broadcast_to(tot_col, (SSM_HEADS, SSM_STATE)),
                         preferred_element_type=F32, precision=HIGHEST)
    st_ref[...] = decay_full * st_ref[...] + jnp.concatenate(ds_groups, axis=0)
    z = z_ref[...]
    v = y * _silu(z)
    outs = []
    for g in range(SSM_GROUPS):
        vg = v[:, g * gw:(g + 1) * gw]
        outs.append(vg * lax.rsqrt(jnp.mean(vg * vg, axis=-1, keepdims=True) + RMS_EPS))
    y_ref[...] = jnp.concatenate(outs, axis=1) * gssm_ref[...]


def _ssd(xconv, dtx, proj, init_state, a_log, d_skip, g_ssm, seq_len):
    rows = xconv.shape[0]
    n_seq = rows // seq_len
    lc = min(SSM_CHUNK, seq_len)
    nc = seq_len // lc
    zblk = COL_Z // SSM_D_INNER
    vec = lambda a: pl.BlockSpec(a.shape, lambda b, c: (0, 0))
    return pl.pallas_call(
        functools.partial(_ssd_kernel, lc=lc),
        grid=(n_seq, nc),
        in_specs=[pl.BlockSpec((lc, CONV_DIM), lambda b, c: (b * nc + c, 0)),
                  pl.BlockSpec((lc, 128), lambda b, c: (b * nc + c, 0)),
                  pl.BlockSpec((lc, SSM_D_INNER), lambda b, c: (b * nc + c, zblk)),
                  pl.BlockSpec((None, SSM_D_INNER, SSM_STATE), lambda b, c: (b, 0, 0)),
                  vec(a_log), vec(d_skip), vec(g_ssm)],
        out_specs=[pl.BlockSpec((lc, SSM_D_INNER), lambda b, c: (b * nc + c, 0)),
                   pl.BlockSpec((None, SSM_D_INNER, SSM_STATE), lambda b, c: (b, 0, 0))],
        out_shape=[jax.ShapeDtypeStruct((rows, SSM_D_INNER), F32),
                   jax.ShapeDtypeStruct((n_seq, SSM_D_INNER, SSM_STATE), F32)],
        compiler_params=_params("parallel", "arbitrary"),
    )(xconv, dtx, proj, init_state, a_log, d_skip, g_ssm)


def _softmax_init(m_scr, l_scr, acc_scr):
    m_scr[...] = jnp.full_like(m_scr, MASK_VALUE)
    l_scr[...] = jnp.zeros_like(l_scr)
    acc_scr[...] = jnp.zeros_like(acc_scr)


def _softmax_step(s, m_scr, l_scr, acc_scr, pv):
    m_prev = m_scr[...]
    m_new = jnp.maximum(m_prev, jnp.max(s, axis=1, keepdims=True))
    alpha = jnp.exp(m_prev - m_new)
    p = jnp.exp(s - m_new)
    l_scr[...] = alpha * l_scr[...] + jnp.sum(p, axis=1, keepdims=True)
    acc_scr[...] = alpha * acc_scr[...] + pv(p.astype(BF16))
    m_scr[...] = m_new


def _mla_prompt_kernel(ql_ref, qr_ref, ckv_ref, kr_ref, wuv_ref, o_ref,
                       ckv_b, kr_b, m_scr, l_scr, acc_scr, *, tq, tk):
    i = pl.program_id(1)

    @pl.when(i == 0)
    def _():
        ckv_b[...] = ckv_ref[...].astype(BF16)
        kr_b[...] = kr_ref[...].astype(BF16)

    rows = MLA_HEADS * tq
    q1 = ql_ref[...].reshape(rows, MLA_KV_RANK).astype(BF16)
    q2 = qr_ref[...].reshape(rows, MLA_ROPE_DIM).astype(BF16)
    _softmax_init(m_scr, l_scr, acc_scr)
    q_pos = i * tq + lax.broadcasted_iota(jnp.int32, (rows, tk), 0) % tq
    key_off = lax.broadcasted_iota(jnp.int32, (rows, tk), 1)

    def body(j, carry):
        ks = pl.multiple_of(j * tk, tk)
        kc = ckv_b[pl.ds(ks, tk), :]
        s = (_dot_nt(q1, kc) + _dot_nt(q2, kr_b[pl.ds(ks, tk), :])) * MLA_SCALE
        s = jnp.where(key_off + ks <= q_pos, s, MASK_VALUE)
        _softmax_step(s, m_scr, l_scr, acc_scr, lambda p: _dot(p, kc))
        return carry

    lax.fori_loop(0, (i * tq) // tk + 1, body, 0)
    lat = (acc_scr[...] / l_scr[...]).astype(BF16)
    for h in range(MLA_HEADS):
        o_ref[:, h * MLA_V_DIM:(h + 1) * MLA_V_DIM] = _dot(lat[h * tq:(h + 1) * tq], wuv_ref[h])


def _mla_prompt(q_lat, q_rope, ckv, krope, w_uv_b, n_seq, seq_len):
    tq = 128
    tk = min(512, seq_len)
    nt = seq_len // tq
    rows = n_seq * seq_len
    return pl.pallas_call(
        functools.partial(_mla_prompt_kernel, tq=tq, tk=tk),
        grid=(n_seq, nt),
        in_specs=[pl.BlockSpec((MLA_HEADS, tq, MLA_KV_RANK), lambda b, i: (0, b * nt + i, 0)),
                  pl.BlockSpec((MLA_HEADS, tq, MLA_ROPE_DIM), lambda b, i: (0, b * nt + i, 0)),
                  pl.BlockSpec((seq_len, MLA_KV_RANK), lambda b, i: (b, 0)),
                  pl.BlockSpec((seq_len, MLA_ROPE_DIM), lambda b, i: (b, 0)),
                  pl.BlockSpec(w_uv_b.shape, lambda b, i: (0, 0, 0))],
        out_specs=pl.BlockSpec((tq, MLA_HEADS * MLA_V_DIM), lambda b, i: (b * nt + i, 0)),
        out_shape=jax.ShapeDtypeStruct((rows, MLA_HEADS * MLA_V_DIM), F32),
        scratch_shapes=[pltpu.VMEM((seq_len, MLA_KV_RANK), BF16), pltpu.VMEM((seq_len, MLA_ROPE_DIM), BF16),
                        pltpu.VMEM((MLA_HEADS * tq, 1), F32), pltpu.VMEM((MLA_HEADS * tq, 1), F32),
                        pltpu.VMEM((MLA_HEADS * tq, MLA_KV_RANK), F32)],
        compiler_params=_params("parallel", "arbitrary"),
    )(q_lat, q_rope, ckv, krope, w_uv_b)


def _moba_select(gate, n_valid):
    nb = gate.shape[1]
    blk = lax.broadcasted_iota(jnp.int32, gate.shape, 1)
    valid = blk < n_valid
    g = jnp.where(valid, gate, -jnp.inf)
    rank = jnp.zeros(gate.shape, jnp.int32)
    for mcol in range(nb):
        gm = g[:, mcol:mcol + 1]
        ahead = (gm > g) | ((gm == g) & (mcol < blk))
        rank = rank + jnp.where(ahead, 1, 0)
    return jnp.where(valid & (rank < MOBA_TOPK), 1.0, 0.0).astype(F32)


def _moba_prompt_kernel(q_ref, k_ref, v_ref, o_ref, k_b, v_b, kmean_scr, m_scr, l_scr, acc_scr, *, nb):
    i = pl.program_id(1)
    tq = MOBA_BLOCK
    rows = MOBA_HEADS * tq

    @pl.when(i == 0)
    def _():
        k = k_ref[...]
        k_b[...] = k.astype(BF16)
        v_b[...] = v_ref[...].astype(BF16)
        kmean_scr[...] = jnp.mean(k.reshape(nb, MOBA_BLOCK, MOBA_HEAD_DIM), axis=1)

    q = q_ref[...].reshape(rows, MOBA_HEAD_DIM)
    qb = q.astype(BF16)
    gate = _dot_nt(q, kmean_scr[...], precision=HIGHEST)
    sel = _moba_select(gate, i)
    blk = lax.broadcasted_iota(jnp.int32, (rows, nb), 1)
    _softmax_init(m_scr, l_scr, acc_scr)

    def attend(j, mask_fn):
        ks = pl.multiple_of(j * MOBA_BLOCK, MOBA_BLOCK)
        s = _dot_nt(qb, k_b[pl.ds(ks, MOBA_BLOCK), :]) * MOBA_SCALE
        s = jnp.where(mask_fn(), s, MASK_VALUE)
        _softmax_step(s, m_scr, l_scr, acc_scr, lambda p: _dot(p, v_b[pl.ds(ks, MOBA_BLOCK), :]))

    t_q = lax.broadcasted_iota(jnp.int32, (rows, MOBA_BLOCK), 0) % tq
    t_k = lax.broadcasted_iota(jnp.int32, (rows, MOBA_BLOCK), 1)
    attend(i, lambda: t_k <= t_q)

    def body(j, carry):
        picked = jnp.sum(jnp.where(blk == j, sel, 0.0), axis=1, keepdims=True) > 0.0
        attend(j, lambda: jnp.broadcast_to(picked, (rows, MOBA_BLOCK)))
        return carry

    lax.fori_loop(0, i, body, 0)
    out = acc_scr[...] / l_scr[...]
    for h in range(MOBA_HEADS):
        o_ref[:, h * MOBA_HEAD_DIM:(h + 1) * MOBA_HEAD_DIM] = out[h * tq:(h + 1) * tq]


def _moba_prompt(qm, km, proj, n_seq, seq_len):
    nb = seq_len // MOBA_BLOCK
    rows = n_seq * seq_len
    vblk = COL_VM // MOBA_HEAD_DIM
    return pl.pallas_call(
        functools.partial(_moba_prompt_kernel, nb=nb),
        grid=(n_seq, nb),
        in_specs=[pl.BlockSpec((MOBA_HEADS, MOBA_BLOCK, MOBA_HEAD_DIM), lambda b, i: (0, b * nb + i, 0)),
                  pl.BlockSpec((seq_len, MOBA_HEAD_DIM), lambda b, i: (b, 0)),
                  pl.BlockSpec((seq_len, MOBA_HEAD_DIM), lambda b, i: (b, vblk))],
        out_specs=pl.BlockSpec((MOBA_BLOCK, MOBA_HEADS * MOBA_HEAD_DIM), lambda b, i: (b * nb + i, 0)),
        out_shape=jax.ShapeDtypeStruct((rows, MOBA_HEADS * MOBA_HEAD_DIM), F32),
        scratch_shapes=[pltpu.VMEM((seq_len, MOBA_HEAD_DIM), BF16), pltpu.VMEM((seq_len, MOBA_HEAD_DIM), BF16),
                        pltpu.VMEM((nb, MOBA_HEAD_DIM), F32),
                        pltpu.VMEM((MOBA_HEADS * MOBA_BLOCK, 1), F32), pltpu.VMEM((MOBA_HEADS * MOBA_BLOCK, 1), F32),
                        pltpu.VMEM((MOBA_HEADS * MOBA_BLOCK, MOBA_HEAD_DIM), F32)],
        compiler_params=_params("parallel", "arbitrary"),
    )(qm, km, proj)


def _mla_sample_kernel(pt_ref, *refs, t_new, pps):
    ckv_pages = refs[0:pps]
    kr_pages = refs[pps:2 * pps]
    mk_pages = refs[2 * pps:3 * pps]
    ql_ref, qr_ref, cnew_ref, knew_ref, wuv_ref, o_ref, kmean_ref, m_scr, l_scr, acc_scr = refs[3 * pps:]
    g = pl.program_id(1)
    rows = MLA_HEADS * t_new

    @pl.when(g == 0)
    def _():
        _softmax_init(m_scr, l_scr, acc_scr)

    q1 = ql_ref[...].reshape(rows, MLA_KV_RANK).astype(BF16)
    q2 = qr_ref[...].reshape(rows, MLA_ROPE_DIM).astype(BF16)
    keys = [r[...].astype(BF16) for r in ckv_pages]
    s = jnp.concatenate([_dot_nt(q1, keys[j]) + _dot_nt(q2, kr_pages[j][...].astype(BF16)) for j in range(pps)],
                        axis=1) * MLA_SCALE

    def pv(p):
        acc = _dot(p[:, 0:PAGE_SIZE], keys[0])
        for j in range(1, pps):
            acc = acc + _dot(p[:, j * PAGE_SIZE:(j + 1) * PAGE_SIZE], keys[j])
        return acc

    _softmax_step(s, m_scr, l_scr, acc_scr, pv)
    pages_per_block = MOBA_BLOCK // PAGE_SIZE
    means = [sum(jnp.sum(mk_pages[b * pages_per_block + u][...], axis=0, keepdims=True)
                 for u in range(pages_per_block)) * (1.0 / MOBA_BLOCK)
             for b in range(pps // pages_per_block)]
    nbs = pps // pages_per_block
    kmean_ref[pl.ds(pl.multiple_of(g * nbs, nbs), nbs), :] = jnp.concatenate(means, axis=0)

    @pl.when(g == pl.num_programs(1) - 1)
    def _():
        cn = cnew_ref[...].astype(BF16)
        sn = (_dot_nt(q1, cn) + _dot_nt(q2, knew_ref[...].astype(BF16))) * MLA_SCALE
        t_q = lax.broadcasted_iota(jnp.int32, (rows, t_new), 0) % t_new
        t_k = lax.broadcasted_iota(jnp.int32, (rows, t_new), 1)
        sn = jnp.where(t_k <= t_q, sn, MASK_VALUE)
        _softmax_step(sn, m_scr, l_scr, acc_scr, lambda p: _dot(p, cn))
        lat = (acc_scr[...] / l_scr[...]).astype(BF16)
        for h in range(MLA_HEADS):
            o_ref[:, h * MLA_V_DIM:(h + 1) * MLA_V_DIM] = _dot(lat[h * t_new:(h + 1) * t_new], wuv_ref[h])


def _mla_sample(page_table, ckv_pool, kr_pool, mk_pool, layer, q_lat, q_rope, ckv_new, kr_new, w_uv_b, t_new):
    n_seq, n_pages = page_table.shape
    pps = min(PAGES_PER_STEP, n_pages)
    steps = n_pages // pps
    nb_past = n_pages * PAGE_SIZE // MOBA_BLOCK

    def page_spec(width, j):
        return pl.BlockSpec((None, None, PAGE_SIZE, width), lambda s, g, pt, j=j: (layer, pt[s, g * pps + j], 0, 0))

    in_specs = ([page_spec(MLA_KV_RANK, j) for j in range(pps)] + [page_spec(MLA_ROPE_DIM, j) for j in range(pps)]
                + [page_spec(MOBA_HEAD_DIM, j) for j in range(pps)]
                + [pl.BlockSpec((MLA_HEADS, t_new, MLA_KV_RANK), lambda s, g, pt: (0, s, 0)),
                   pl.BlockSpec((MLA_HEADS, t_new, MLA_ROPE_DIM), lambda s, g, pt: (0, s, 0)),
                   pl.BlockSpec((t_new, MLA_KV_RANK), lambda s, g, pt: (s, 0)),
                   pl.BlockSpec((t_new, MLA_ROPE_DIM), lambda s, g, pt: (s, 0)),
                   pl.BlockSpec(w_uv_b.shape, lambda s, g, pt: (0, 0, 0))])
    grid_spec = pltpu.PrefetchScalarGridSpec(
        num_scalar_prefetch=1, grid=(n_seq, steps), in_specs=in_specs,
        out_specs=[pl.BlockSpec((t_new, MLA_HEADS * MLA_V_DIM), lambda s, g, pt: (s, 0)),
                   pl.BlockSpec((None, nb_past, MOBA_HEAD_DIM), lambda s, g, pt: (s, 0, 0))],
        scratch_shapes=[pltpu.VMEM((MLA_HEADS * t_new, 1), F32), pltpu.VMEM((MLA_HEADS * t_new, 1), F32),
                        pltpu.VMEM((MLA_HEADS * t_new, MLA_KV_RANK), F32)])
    return pl.pallas_call(
        functools.partial(_mla_sample_kernel, t_new=t_new, pps=pps),
        grid_spec=grid_spec,
        out_shape=[jax.ShapeDtypeStruct((n_seq * t_new, MLA_HEADS * MLA_V_DIM), F32),
                   jax.ShapeDtypeStruct((n_seq, nb_past, MOBA_HEAD_DIM), F32)],
        compiler_params=_params("parallel", "arbitrary"),
    )(page_table, *([ckv_pool] * pps), *([kr_pool] * pps), *([mk_pool] * pps),
      q_lat, q_rope, ckv_new, kr_new, w_uv_b)


def _moba_sample_kernel(pt_ref, *refs, t_new, pps, nb_past):
    k_pages = refs[0:pps]
    v_pages = refs[pps:2 * pps]
    q_ref, kmean_ref, knew_ref, vnew_ref, o_ref, sel_scr, m_scr, l_scr, acc_scr = refs[2 * pps:]
    g = pl.program_id(1)
    rows = MOBA_HEADS * t_new
    q = q_ref[...].reshape(rows, MOBA_HEAD_DIM)
    qb = q.astype(BF16)

    @pl.when(g == 0)
    def _():
        _softmax_init(m_scr, l_scr, acc_scr)
        kn = knew_ref[...].astype(BF16)
        vn = vnew_ref[...].astype(BF16)
        sn = _dot_nt(qb, kn) * MOBA_SCALE
        t_q = lax.broadcasted_iota(jnp.int32, (rows, t_new), 0) % t_new
        t_k = lax.broadcasted_iota(jnp.int32, (rows, t_new), 1)
        sn = jnp.where(t_k <= t_q, sn, MASK_VALUE)
        _softmax_step(sn, m_scr, l_scr, acc_scr, lambda p: _dot(p, vn))
        gate = _dot_nt(q, kmean_ref[...], precision=HIGHEST)
        sel = _moba_select(gate, nb_past)
        for b in range(nb_past):
            sel_scr[b] = jnp.broadcast_to(sel[:, b:b + 1], (rows, 128))

    pages_per_block = MOBA_BLOCK // PAGE_SIZE
    nbs = pps // pages_per_block
    s_parts = []
    for b in range(nbs):
        picked = sel_scr[g * nbs + b] > 0.0
        for u in range(pages_per_block):
            s_pg = _dot_nt(qb, k_pages[b * pages_per_block + u][...].astype(BF16)) * MOBA_SCALE
            s_parts.append(jnp.where(picked, s_pg, MASK_VALUE))
    s = jnp.concatenate(s_parts, axis=1)

    def pv(p):
        acc = None
        for j in range(pps):
            part = _dot(p[:, j * PAGE_SIZE:(j + 1) * PAGE_SIZE], v_pages[j][...].astype(BF16))
            acc = part if acc is None else acc + part
        return acc

    _softmax_step(s, m_scr, l_scr, acc_scr, pv)

    @pl.when(g == pl.num_programs(1) - 1)
    def _():
        out = acc_scr[...] / l_scr[...]
        for h in range(MOBA_HEADS):
            o_ref[:, h * MOBA_HEAD_DIM:(h + 1) * MOBA_HEAD_DIM] = out[h * t_new:(h + 1) * t_new]


def _moba_sample(page_table, k_pool, v_pool, layer, qm, kmean, km_new, proj, t_new):
    n_seq, n_pages = page_table.shape
    pps = min(PAGES_PER_STEP, n_pages)
    steps = n_pages // pps
    nb_past = n_pages * PAGE_SIZE // MOBA_BLOCK
    vblk = COL_VM // MOBA_HEAD_DIM

    def page_spec(j):
        return pl.BlockSpec((None, None, PAGE_SIZE, MOBA_HEAD_DIM),
                            lambda s, g, pt, j=j: (layer, pt[s, g * pps + j], 0, 0))

    in_specs = ([page_spec(j) for j in range(pps)] + [page_spec(j) for j in range(pps)]
                + [pl.BlockSpec((MOBA_HEADS, t_new, MOBA_HEAD_DIM), lambda s, g, pt: (0, s, 0)),
                   pl.BlockSpec((None, nb_past, MOBA_HEAD_DIM), lambda s, g, pt: (s, 0, 0)),
                   pl.BlockSpec((t_new, MOBA_HEAD_DIM), lambda s, g, pt: (s, 0)),
                   pl.BlockSpec((t_new, MOBA_HEAD_DIM), lambda s, g, pt: (s, vblk))])
    rows = MOBA_HEADS * t_new
    grid_spec = pltpu.PrefetchScalarGridSpec(
        num_scalar_prefetch=1, grid=(n_seq, steps), in_specs=in_specs,
        out_specs=pl.BlockSpec((t_new, MOBA_HEADS * MOBA_HEAD_DIM), lambda s, g, pt: (s, 0)),
        scratch_shapes=[pltpu.VMEM((nb_past, rows, 128), F32),
                        pltpu.VMEM((rows, 1), F32), pltpu.VMEM((rows, 1), F32),
                        pltpu.VMEM((rows, MOBA_HEAD_DIM), F32)])
    return pl.pallas_call(
        functools.partial(_moba_sample_kernel, t_new=t_new, pps=pps, nb_past=nb_past),
        grid_spec=grid_spec,
        out_shape=jax.ShapeDtypeStruct((n_seq * t_new, MOBA_HEADS * MOBA_HEAD_DIM), F32),
        compiler_params=_params("parallel", "arbitrary"),
    )(page_table, *([k_pool] * pps), *([v_pool] * pps), qm, kmean, km_new, proj)


def _outproj_kernel(mla_ref, moba_ref, ssm_ref, w_ref, x_ref, gt_ref, lg_ref, lb_ref, o_ref, *, alpha):
    w_mla = MLA_HEADS * MLA_V_DIM
    w_moba = MOBA_HEADS * MOBA_HEAD_DIM
    mixed = (_dot(mla_ref[...].astype(BF16), w_ref[0:w_mla, :])
             + _dot(moba_ref[...].astype(BF16), w_ref[w_mla:w_mla + w_moba, :])
             + _dot(ssm_ref[...].astype(BF16), w_ref[w_mla + w_moba:, :]))
    v = alpha * x_ref[...] + gt_ref[...] * mixed
    o_ref[...] = _layer_norm_rows(v, lg_ref[...], lb_ref[...])


def _outproj(mla_out, moba_out, ssm_out, w_out_b, layer, x, mod, ln_g, ln_b, alpha, tm):
    rows = x.shape[0]
    row = lambda w: pl.BlockSpec((tm, w), lambda i: (i, 0))
    vec_spec = pl.BlockSpec((1, D_MODEL), lambda i: (0, 0))
    return pl.pallas_call(
        functools.partial(_outproj_kernel, alpha=alpha),
        grid=(rows // tm,),
        in_specs=[row(mla_out.shape[1]), row(moba_out.shape[1]), row(ssm_out.shape[1]),
                  pl.BlockSpec((None, D_MODEL, D_MODEL), lambda i: (layer, 0, 0)),
                  row(D_MODEL), mod.spec(tm), vec_spec, vec_spec],
        out_specs=row(D_MODEL),
        out_shape=jax.ShapeDtypeStruct((rows, D_MODEL), F32),
        compiler_params=_params("parallel"),
    )(mla_out, moba_out, ssm_out, w_out_b, x, mod.arrays[2], ln_g, ln_b)


def _trunk(x, mods_by_layer, seq_len, tabs, conv_init, ssm_init, attn, w, tm):
    depth = w["w_in_p"].shape[0]
    alpha = (2 * depth) ** 0.25
    n_seq = x.shape[0] // seq_len
    tab_tiles = tabs[0].shape[0] // tm
    states = []
    for l in range(depth):
        mods = mods_by_layer[l]
        ln_g, ln_b = w["ln_g"][l], w["ln_b"][l]
        x = _ffn(x, mods[0], w["wg"], w["wu"], w["wd"], ln_g[0:1], ln_b[0:1], l, 0, alpha, tm)
        proj = _inproj(x, mods[1], w["w_in_p"], l, tm)
        q_lat, q_rope, ckv, krope, qm, km, dtx = _prep(
            proj, tabs, tab_tiles, w["g_q"][l:l + 1], w["w_uq_p"][l], w["g_kv"][l:l + 1], w["w_uk_t"][l],
            w["dt_bias_p"][l:l + 1], tm)
        conv_tm = min(tm, seq_len)
        init8 = jnp.pad(conv_init[l], ((0, 0), (8 - (CONV_WIDTH - 1), 0), (0, 0)))
        xconv = _conv(proj, init8, w["conv_w"][l], w["conv_b"][l:l + 1], seq_len, conv_tm)
        ssm_out, ssm_state = _ssd(xconv, dtx, proj, ssm_init[l], w["a_log"][l:l + 1], w["d_skip"][l:l + 1],
                                  w["g_ssm"][l:l + 1], seq_len)
        mla_out, moba_out = attn(l, q_lat, q_rope, ckv, krope, qm, km, proj)
        x = _outproj(mla_out, moba_out, ssm_out, w["w_out_b"], l, x, mods[1], ln_g[1:2], ln_b[1:2], alpha, tm)
        x = _ffn(x, mods[2], w["wg"], w["wu"], w["wd"], ln_g[2:3], ln_b[2:3], l, 1, alpha, tm)
        vm = proj[:, COL_VM:COL_VM + MOBA_HEAD_DIM]
        xbc = proj[:, COL_XBC:COL_XBC + CONV_DIM].reshape(n_seq, seq_len, CONV_DIM)
        conv_state = jnp.concatenate([conv_init[l], xbc], axis=1)[:, -(CONV_WIDTH - 1):]
        states.append((ckv.reshape(n_seq, seq_len, MLA_KV_RANK),
                       krope.reshape(n_seq, seq_len, MLA_ROPE_DIM),
                       km.reshape(n_seq, seq_len, 1, MOBA_HEAD_DIM),
                       vm.reshape(n_seq, seq_len, 1, MOBA_HEAD_DIM),
                       ssm_state.reshape(n_seq, SSM_HEADS, SSM_HEAD_DIM, SSM_STATE),
                       conv_state))
    return x, [jnp.stack([s[i] for s in states]) for i in range(len(states[0]))]


def _row_tile(rows, seq_len, largest):
    for tm in (512, 256, 128, 64, 32, 16, 8):
        if tm <= largest and rows % tm == 0 and (seq_len % tm == 0 or tm % seq_len == 0):
            return tm
    raise ValueError("row count must be a multiple of 8")


def kernel(x_prompt, x_sample, cache_mla_ckv, cache_mla_krope, cache_moba_k, cache_moba_v, state_ssm, state_conv,
           page_table, c_prompt, c_sample, w_ada, b_ada, ln_g, ln_b, ffn_w_gate, ffn_w_up, ffn_w_down,
           w_in, g_q, w_uq, g_kv, w_uk, w_uv, conv_w, conv_b, dt_bias, a_log, d_skip, g_ssm, w_out):
    depth = w_in.shape[0]
    n_prompt, seq_len, _ = x_prompt.shape
    n_sample, t_new, _ = x_sample.shape
    n_pages = page_table.shape[1]
    past_len = n_pages * PAGE_SIZE
    n_pool = cache_mla_ckv.shape[1]

    assert past_len % MOBA_BLOCK == 0 and t_new <= MOBA_BLOCK and seq_len % MOBA_BLOCK == 0
    names = ["cq", "ckv", "kr", "qm", "km", "vm", "z", "xbc", "dt"]
    widths = [MLA_Q_RANK, MLA_KV_RANK, MLA_ROPE_DIM, MOBA_HEADS * MOBA_HEAD_DIM, MOBA_HEAD_DIM, MOBA_HEAD_DIM,
              SSM_D_INNER, CONV_DIM, SSM_HEADS]
    pieces, start = {}, 0
    for name, width in zip(names, widths):
        pieces[name] = (start, start + width)
        start += width
    order = ["cq", "qm", "ckv", "km", "vm", "xbc", "z", "kr", "dt"]
    w_in_p = jnp.concatenate([w_in[:, :, pieces[k][0]:pieces[k][1]] for k in order]
                             + [jnp.zeros((depth, D_MODEL, IN_COLS_PADDED - w_in.shape[2]), w_in.dtype)],
                             axis=2).astype(BF16)
    qk = MLA_NOPE_DIM + MLA_ROPE_DIM
    w_uq_h = w_uq.reshape(depth, MLA_Q_RANK, MLA_HEADS, qk)
    w_uq_p = jnp.concatenate([w_uq_h[..., :MLA_NOPE_DIM].reshape(depth, MLA_Q_RANK, MLA_HEADS * MLA_NOPE_DIM),
                              w_uq_h[..., MLA_NOPE_DIM:].reshape(depth, MLA_Q_RANK, MLA_HEADS * MLA_ROPE_DIM)],
                             axis=2).astype(BF16)
    dt_bias_p = jnp.zeros((depth, 128), F32).at[:, DT_LANE:DT_LANE + SSM_HEADS].set(dt_bias)
    w = dict(
        wg=ffn_w_gate.astype(BF16), wu=ffn_w_up.astype(BF16), wd=ffn_w_down.astype(BF16),
        w_in_p=w_in_p, w_uq_p=w_uq_p, w_uk_t=jnp.swapaxes(w_uk, 2, 3).astype(BF16), w_uv_b=w_uv.astype(BF16),
        w_out_b=w_out.astype(BF16), g_q=g_q, g_kv=g_kv, ln_g=ln_g, ln_b=ln_b,
        conv_w=conv_w.reshape(depth, CONV_WIDTH, CONV_DIM), conv_b=conv_b, dt_bias_p=dt_bias_p,
        a_log=a_log, d_skip=d_skip, g_ssm=g_ssm)

    n_c = n_prompt + n_sample
    c_rows = -(-n_c // 8) * 8
    c_all = jnp.concatenate([c_prompt, c_sample, jnp.zeros((c_rows - n_c, D_MODEL), F32)], axis=0)
    mod = _ada(c_all, w_ada, b_ada)
    mods_p = [_make_mods(mod[l, :n_prompt], seq_len, per_token=False) for l in range(depth)]
    mods_s = [_make_mods(mod[l, n_prompt:n_c], t_new, per_token=True) for l in range(depth)]

    tabs_p = _rope_tables(jnp.arange(seq_len, dtype=jnp.int32))
    conv0 = jnp.zeros((depth, n_prompt, CONV_WIDTH - 1, CONV_DIM), F32)
    ssm0 = jnp.zeros((depth, n_prompt, SSM_D_INNER, SSM_STATE), F32)

    def attn_prompt(l, q_lat, q_rope, ckv, krope, qm, km, proj):
        return (_mla_prompt(q_lat, q_rope, ckv, krope, w["w_uv_b"][l], n_prompt, seq_len),
                _moba_prompt(qm, km, proj, n_prompt, seq_len))

    tm_p = _row_tile(n_prompt * seq_len, seq_len, 512)
    y_p, st_p = _trunk(x_prompt.reshape(n_prompt * seq_len, D_MODEL), mods_p, seq_len, tabs_p, conv0, ssm0,
                       attn_prompt, w, tm_p)

    pos_s = past_len + jnp.arange(t_new, dtype=jnp.int32)
    tabs_s = tuple(jnp.tile(t, (n_sample, 1)) for t in _rope_tables(pos_s))
    mk_pool = cache_moba_k.reshape(depth, n_pool, PAGE_SIZE, MOBA_HEAD_DIM)
    mv_pool = cache_moba_v.reshape(depth, n_pool, PAGE_SIZE, MOBA_HEAD_DIM)

    def attn_sample(l, q_lat, q_rope, ckv, krope, qm, km, proj):
        mla_out, kmean = _mla_sample(page_table, cache_mla_ckv, cache_mla_krope, mk_pool, l, q_lat, q_rope,
                                     ckv, krope, w["w_uv_b"][l], t_new)
        moba_out = _moba_sample(page_table, mk_pool, mv_pool, l, qm, kmean, km, proj, t_new)
        return mla_out, moba_out

    tm_s = _row_tile(n_sample * t_new, t_new, 256)
    ssm_init_s = state_ssm.reshape(depth, n_sample, SSM_D_INNER, SSM_STATE)
    y_s, st_s = _trunk(x_sample.reshape(n_sample * t_new, D_MODEL), mods_s, t_new, tabs_s, state_conv, ssm_init_s,
                       attn_sample, w, tm_s)

    return (y_p.reshape(n_prompt, seq_len, D_MODEL), y_s.reshape(n_sample, t_new, D_MODEL),
            *st_p, *st_s)
```

```python
import functools
import math

import jax
import jax.numpy as jnp
from jax import lax
from jax.experimental import pallas as pl
from jax.experimental.pallas import tpu as pltpu

F32 = jnp.float32
BF16 = jnp.bfloat16

D_MODEL = 2048
PAGE_SIZE = 128
MLA_HEADS = 4
MLA_Q_RANK = 512
MLA_KV_RANK = 256
MLA_NOPE_DIM = 128
MLA_ROPE_DIM = 64
MLA_V_DIM = 128
MLA_SCALE = (MLA_NOPE_DIM + MLA_ROPE_DIM) ** -0.5
MOBA_HEADS = 4
MOBA_HEAD_DIM = 128
MOBA_BLOCK = 256
MOBA_TOPK = 3
MOBA_SCALE = MOBA_HEAD_DIM ** -0.5
SSM_HEADS = 16
SSM_HEAD_DIM = 64
SSM_D_INNER = SSM_HEADS * SSM_HEAD_DIM
SSM_GROUPS = 2
SSM_STATE = 128
SSM_CHUNK = 128
CONV_WIDTH = 4
CONV_DIM = SSM_D_INNER + 2 * SSM_GROUPS * SSM_STATE
MACARON_WEIGHT = 0.5
ROPE_THETA = 10000.0
LN_EPS = 1e-5
RMS_EPS = 1e-6

COL_CQ = 0
COL_QM = 512
COL_CKV = 1024
COL_KM = 1280
COL_VM = 1408
COL_XBC = 1536
COL_Z = 3072
COL_KRDT = 4096
IN_COLS_PADDED = 4224
DT_LANE = 64

MASK_VALUE = -1e30
VMEM_LIMIT = 56 * 1024 * 1024

HIGHEST = lax.Precision.HIGHEST
NT_DIMS = (((1,), (1,)), ((), ()))
TN_DIMS = (((0,), (0,)), ((), ()))


def _params(*sem):
    return pltpu.CompilerParams(dimension_semantics=sem, vmem_limit_bytes=VMEM_LIMIT)


def _silu(v):
    return v * jax.nn.sigmoid(v)


def _dot(a, b):
    return jnp.dot(a, b, preferred_element_type=F32)


def _dot_nt(a, b, precision=None):
    return lax.dot_general(a, b, NT_DIMS, preferred_element_type=F32, precision=precision)


def _layer_norm_rows(v, g, b):
    mu = jnp.mean(v, axis=-1, keepdims=True)
    d = v - mu
    var = jnp.mean(d * d, axis=-1, keepdims=True)
    return d * lax.rsqrt(var + LN_EPS) * g + b


def _ada_kernel(c_ref, w_ref, b_ref, o_ref):
    h = _silu(c_ref[...]).astype(BF16)
    o_ref[...] = _dot(h, w_ref[...].astype(BF16)) + b_ref[...]


def _ada(c_all, w_ada, b_ada):
    depth, d, n = w_ada.shape
    rows = c_all.shape[0]
    tn = 1024
    return pl.pallas_call(
        _ada_kernel,
        grid=(depth, n // tn),
        in_specs=[
            pl.BlockSpec((rows, d), lambda l, j: (0, 0)),
            pl.BlockSpec((None, d, tn), lambda l, j: (l, 0, j)),
            pl.BlockSpec((None, 1, tn), lambda l, j: (l, 0, j)),
        ],
        out_specs=pl.BlockSpec((None, rows, tn), lambda l, j: (l, 0, j)),
        out_shape=jax.ShapeDtypeStruct((depth, rows, n), F32),
        compiler_params=_params("parallel", "parallel"),
    )(c_all, w_ada, b_ada.reshape(depth, 1, n))


class _Mod:
    def __init__(self, arrays, per_token, tiles_per_seq):
        self.arrays = arrays
        self.per_token = per_token
        self.tiles_per_seq = tiles_per_seq

    def spec(self, tm):
        if self.per_token:
            return pl.BlockSpec((tm, D_MODEL), lambda i, *_: (i, 0))
        tps = self.tiles_per_seq(tm)
        return pl.BlockSpec((None, 1, D_MODEL), lambda i, *_: (i // tps, 0, 0))


def _make_mods(mod, seq_len, per_token):
    n_seq = mod.shape[0]
    m = mod.reshape(n_seq, 3, 3, D_MODEL)
    out = []
    for j in range(3):
        arrays = []
        for k in range(3):
            a = m[:, j, k]
            if per_token:
                a = jnp.broadcast_to(a[:, None, :], (n_seq, seq_len, D_MODEL)).reshape(n_seq * seq_len, D_MODEL)
            else:
                a = a[:, None, :]
            arrays.append(a)
        out.append(_Mod(arrays, per_token, lambda tm: seq_len // tm))
    return out


def _ffn_kernel(x_ref, sh_ref, sc_ref, gt_ref, wg_ref, wu_ref, wd_ref, lg_ref, lb_ref, o_ref,
                h_scr, acc_scr, *, alpha, weight):
    j = pl.program_id(1)

    @pl.when(j == 0)
    def _():
        h_scr[...] = (x_ref[...] * (1.0 + sc_ref[...]) + sh_ref[...]).astype(BF16)
        acc_scr[...] = jnp.zeros_like(acc_scr)

    h = h_scr[...]
    a = _dot(h, wg_ref[...])
    u = _dot(h, wu_ref[...])
    act = (_silu(a) * u).astype(BF16)
    acc_scr[...] += _dot(act, wd_ref[...])

    @pl.when(j == pl.num_programs(1) - 1)
    def _():
        v = alpha * x_ref[...] + weight * gt_ref[...] * acc_scr[...]
        o_ref[...] = _layer_norm_rows(v, lg_ref[...], lb_ref[...])


def _ffn(x, mod, wg, wu, wd, ln_g, ln_b, layer, slot, alpha, tm):
    rows = x.shape[0]
    d_ff = wg.shape[-1]
    tf = 512 if d_ff % 512 == 0 else d_ff
    row_spec = pl.BlockSpec((tm, D_MODEL), lambda i, j: (i, 0))
    vec_spec = pl.BlockSpec((1, D_MODEL), lambda i, j: (0, 0))
    return pl.pallas_call(
        functools.partial(_ffn_kernel, alpha=alpha, weight=MACARON_WEIGHT),
        grid=(rows // tm, d_ff // tf),
        in_specs=[row_spec, mod.spec(tm), mod.spec(tm), mod.spec(tm),
                  pl.BlockSpec((None, None, D_MODEL, tf), lambda i, j: (layer, slot, 0, j)),
                  pl.BlockSpec((None, None, D_MODEL, tf), lambda i, j: (layer, slot, 0, j)),
                  pl.BlockSpec((None, None, tf, D_MODEL), lambda i, j: (layer, slot, j, 0)),
                  vec_spec, vec_spec],
        out_specs=row_spec,
        out_shape=jax.ShapeDtypeStruct((rows, D_MODEL), F32),
        scratch_shapes=[pltpu.VMEM((tm, D_MODEL), BF16), pltpu.VMEM((tm, D_MODEL), F32)],
        compiler_params=_params("parallel", "arbitrary"),
    )(x, mod.arrays[0], mod.arrays[1], mod.arrays[2], wg, wu, wd, ln_g, ln_b)


def _inproj_kernel(x_ref, sh_ref, sc_ref, w_ref, o_ref):
    h = (x_ref[...] * (1.0 + sc_ref[...]) + sh_ref[...]).astype(BF16)
    o_ref[...] = _dot(h, w_ref[...])


def _inproj(x, mod, w_in_p, layer, tm):
    rows = x.shape[0]
    tn = IN_COLS_PADDED // 3
    return pl.pallas_call(
        _inproj_kernel,
        grid=(rows // tm, 3),
        in_specs=[pl.BlockSpec((tm, D_MODEL), lambda i, j: (i, 0)), mod.spec(tm), mod.spec(tm),
                  pl.BlockSpec((None, D_MODEL, tn), lambda i, j: (layer, 0, j))],
        out_specs=pl.BlockSpec((tm, tn), lambda i, j: (i, j)),
        out_shape=jax.ShapeDtypeStruct((rows, IN_COLS_PADDED), F32),
        compiler_params=_params("parallel", "arbitrary"),
    )(x, mod.arrays[0], mod.arrays[1], w_in_p)


def _rope64(v, cos, sin):
    lane = lax.broadcasted_iota(jnp.int32, v.shape, 1)
    swap = jnp.where(lane % 64 < 32, pltpu.roll(v, 96, axis=1), pltpu.roll(v, 32, axis=1))
    return v * cos + swap * sin


def _rope128(v, cos, sin):
    return v * cos + pltpu.roll(v, 64, axis=1) * sin


def _rms(v, g):
    return v * lax.rsqrt(jnp.mean(v * v, axis=-1, keepdims=True) + RMS_EPS) * g


def _prep_kernel(cq_ref, qm_ref, ckvr_ref, km_ref, krdt_ref, cosa_ref, sina_ref, cosb_ref, sinb_ref,
                 gq_ref, wuq_ref, gkv_ref, wuk_ref, dtb_ref,
                 qlat_ref, qrope_ref, ckv_ref, krope_ref, qmo_ref, kmo_ref, dtx_ref):
    cosa, sina = cosa_ref[...], sina_ref[...]
    cosb, sinb = cosb_ref[...], sinb_ref[...]
    cqn = _rms(cq_ref[...], gq_ref[...]).astype(BF16)
    q = _dot(cqn, wuq_ref[...])
    nope_w = MLA_HEADS * MLA_NOPE_DIM
    for pair in range(MLA_HEADS // 2):
        r = _rope64(q[:, nope_w + pair * 128: nope_w + (pair + 1) * 128], cosa, sina)
        qrope_ref[2 * pair] = r[:, :MLA_ROPE_DIM]
        qrope_ref[2 * pair + 1] = r[:, MLA_ROPE_DIM:]
    for h in range(MLA_HEADS):
        qn = q[:, h * MLA_NOPE_DIM:(h + 1) * MLA_NOPE_DIM].astype(BF16)
        qlat_ref[h] = _dot(qn, wuk_ref[h])
    ckv_ref[...] = _rms(ckvr_ref[...], gkv_ref[...])
    krdt = krdt_ref[...]
    krope_ref[...] = _rope64(krdt, cosa, sina)[:, :MLA_ROPE_DIM]
    dtx_ref[...] = jax.nn.softplus(krdt + dtb_ref[...])
    qm = qm_ref[...]
    for h in range(MOBA_HEADS):
        qmo_ref[h] = _rope128(qm[:, h * MOBA_HEAD_DIM:(h + 1) * MOBA_HEAD_DIM], cosb, sinb)
    kmo_ref[...] = _rope128(km_ref[...], cosb, sinb)


def _prep(proj, tabs, tab_tiles, g_q, w_uq_p, g_kv, w_uk_t, dt_bias_p, tm):
    rows = proj.shape[0]

    def col(width, start):
        return pl.BlockSpec((tm, width), lambda i: (i, start // width))

    tab_spec = pl.BlockSpec((tm, 128), lambda i: (i % tab_tiles, 0))

    def full(a):
        return pl.BlockSpec(a.shape, lambda i: (0,) * a.ndim)

    heads_spec = lambda w: pl.BlockSpec((MLA_HEADS, tm, w), lambda i: (0, i, 0))
    row_spec = lambda w: pl.BlockSpec((tm, w), lambda i: (i, 0))
    return pl.pallas_call(
        _prep_kernel,
        grid=(rows // tm,),
        in_specs=[col(512, COL_CQ), col(512, COL_QM), col(256, COL_CKV), col(128, COL_KM), col(128, COL_KRDT),
                  tab_spec, tab_spec, tab_spec, tab_spec,
                  full(g_q), full(w_uq_p), full(g_kv), full(w_uk_t), full(dt_bias_p)],
        out_specs=[heads_spec(MLA_KV_RANK), heads_spec(MLA_ROPE_DIM), row_spec(MLA_KV_RANK), row_spec(MLA_ROPE_DIM),
                   heads_spec(MOBA_HEAD_DIM), row_spec(MOBA_HEAD_DIM), row_spec(128)],
        out_shape=[jax.ShapeDtypeStruct((MLA_HEADS, rows, MLA_KV_RANK), F32),
                   jax.ShapeDtypeStruct((MLA_HEADS, rows, MLA_ROPE_DIM), F32),
                   jax.ShapeDtypeStruct((rows, MLA_KV_RANK), F32),
                   jax.ShapeDtypeStruct((rows, MLA_ROPE_DIM), F32),
                   jax.ShapeDtypeStruct((MOBA_HEADS, rows, MOBA_HEAD_DIM), F32),
                   jax.ShapeDtypeStruct((rows, MOBA_HEAD_DIM), F32),
                   jax.ShapeDtypeStruct((rows, 128), F32)],
        compiler_params=_params("parallel"),
    )(proj, proj, proj, proj, proj, *tabs, g_q, w_uq_p, g_kv, w_uk_t, dt_bias_p)


def _rope_tables(pos):
    def tab(half):
        inv = ROPE_THETA ** (-jnp.arange(half, dtype=F32) / half)
        ang = pos.astype(F32)[:, None] * inv[None, :]
        return jnp.cos(ang), jnp.sin(ang)

    c32, s32 = tab(MLA_ROPE_DIM // 2)
    c64, s64 = tab(MOBA_HEAD_DIM // 2)
    cosa = jnp.concatenate([c32, c32, c32, c32], axis=1)
    sina = jnp.concatenate([-s32, s32, -s32, s32], axis=1)
    cosb = jnp.concatenate([c64, c64], axis=1)
    sinb = jnp.concatenate([-s64, s64], axis=1)
    return cosa, sina, cosb, sinb


def _conv_kernel(cur_ref, prev_ref, init_ref, w_ref, b_ref, o_ref, pad_scr, *, tiles_per_seq, tm):
    i = pl.program_id(0)
    at_start = (i % tiles_per_seq) == 0
    pad_scr[0:8, :] = jnp.where(at_start, init_ref[...], prev_ref[...])
    pad_scr[8:8 + tm, :] = cur_ref[...]
    acc = b_ref[...] + w_ref[CONV_WIDTH - 1:CONV_WIDTH, :] * cur_ref[...]
    for k in range(CONV_WIDTH - 1):
        shift = CONV_WIDTH - 1 - k
        acc = acc + w_ref[k:k + 1, :] * pad_scr[8 - shift:8 - shift + tm, :]
    o_ref[...] = _silu(acc)


def _conv(proj, init8, conv_w, conv_b, seq_len, tm):
    rows = proj.shape[0]
    tiles_per_seq = seq_len // tm
    blk = COL_XBC // CONV_DIM
    return pl.pallas_call(
        functools.partial(_conv_kernel, tiles_per_seq=tiles_per_seq, tm=tm),
        grid=(rows // tm,),
        in_specs=[pl.BlockSpec((tm, CONV_DIM), lambda i: (i, blk)),
                  pl.BlockSpec((8, CONV_DIM), lambda i: (jnp.maximum(i * (tm // 8) - 1, 0), blk)),
                  pl.BlockSpec((None, 8, CONV_DIM), lambda i: (i // tiles_per_seq, 0, 0)),
                  pl.BlockSpec((CONV_WIDTH, CONV_DIM), lambda i: (0, 0)),
                  pl.BlockSpec((1, CONV_DIM), lambda i: (0, 0))],
        out_specs=pl.BlockSpec((tm, CONV_DIM), lambda i: (i, 0)),
        out_shape=jax.ShapeDtypeStruct((rows, CONV_DIM), F32),
        scratch_shapes=[pltpu.VMEM((tm + 8, CONV_DIM), F32)],
        compiler_params=_params("arbitrary"),
    )(proj, proj, init8, conv_w, conv_b)


def _ssd_kernel(xc_ref, dtx_ref, z_ref, init_ref, alog_ref, dskip_ref, gssm_ref, y_ref, st_ref, *, lc):
    c = pl.program_id(1)

    @pl.when(c == 0)
    def _():
        st_ref[...] = init_ref[...]

    gw = SSM_D_INNER // SSM_GROUPS
    xs = xc_ref[:, 0:SSM_D_INNER]
    dt = dtx_ref[:, DT_LANE:DT_LANE + SSM_HEADS]
    a_head = -jnp.exp(alog_ref[...])
    adt = dt * a_head
    row = lax.broadcasted_iota(jnp.int32, (lc, lc), 0)
    colm = lax.broadcasted_iota(jnp.int32, (lc, lc), 1)
    tril = row >= colm
    a_cs = jnp.dot(tril.astype(F32), adt, preferred_element_type=F32, precision=HIGHEST)
    eye = (lax.broadcasted_iota(jnp.int32, (SSM_HEADS, SSM_HEADS), 0)
           == lax.broadcasted_iota(jnp.int32, (SSM_HEADS, SSM_HEADS), 1)).astype(F32)
    a_cs_t = _dot_nt(eye, a_cs, precision=HIGHEST)
    a_tot = a_cs[lc - 1:lc, :]
    spread = (lax.broadcasted_iota(jnp.int32, (SSM_HEADS, SSM_D_INNER), 1) // SSM_HEAD_DIM
              == lax.broadcasted_iota(jnp.int32, (SSM_HEADS, SSM_D_INNER), 0)).astype(F32)
    spread_t = (lax.broadcasted_iota(jnp.int32, (SSM_D_INNER, SSM_HEADS), 0) // SSM_HEAD_DIM
                == lax.broadcasted_iota(jnp.int32, (SSM_D_INNER, SSM_HEADS), 1)).astype(F32)

    def expand(v):
        return jnp.dot(v, spread, preferred_element_type=F32, precision=HIGHEST)

    xdt = xs * expand(dt)
    xd_state = (xdt * expand(jnp.exp(a_tot - a_cs))).astype(BF16)
    lane = lax.broadcasted_iota(jnp.int32, (lc, 128), 1)
    y_groups = []
    ds_groups = []
    for g in range(SSM_GROUPS):
        bg = xc_ref[:, SSM_D_INNER + g * SSM_STATE: SSM_D_INNER + (g + 1) * SSM_STATE].astype(BF16)
        cg = xc_ref[:, SSM_D_INNER + (SSM_GROUPS + g) * SSM_STATE:
                    SSM_D_INNER + (SSM_GROUPS + g + 1) * SSM_STATE].astype(BF16)
        cb = _dot_nt(cg, bg)
        s_g = st_ref[g * gw:(g + 1) * gw, :]
        y_off = _dot_nt(cg, s_g.astype(BF16))
        pairs = []
        for pr in range(gw // 128):
            x_pair = xdt[:, g * gw + pr * 128: g * gw + (pr + 1) * 128]
            acc = None
            for sub in range(2):
                h = (g * gw + pr * 128) // SSM_HEAD_DIM + sub
                seg = a_cs[:, h:h + 1] - a_cs_t[h:h + 1, :]
                decay = jnp.exp(jnp.where(tril, seg, -jnp.inf))
                gmat = (cb * decay).astype(BF16)
                x_h = jnp.where((lane // SSM_HEAD_DIM) == sub, x_pair, 0.0).astype(BF16)
                part = _dot(gmat, x_h)
                acc = part if acc is None else acc + part
            pairs.append(acc)
        y_groups.append((jnp.concatenate(pairs, axis=1), y_off))
        ds_groups.append(lax.dot_general(xd_state[:, g * gw:(g + 1) * gw], bg, TN_DIMS,
                                         preferred_element_type=F32))
    y_diag = jnp.concatenate([p[0] for p in y_groups], axis=1)
    y_off = jnp.concatenate([p[1] for p in y_groups], axis=1)
    y = y_diag + y_off * expand(jnp.exp(a_cs)) + expand(dskip_ref[...]) * xs
    tot_col = jnp.exp(a_cs_t[:, lc - 1:lc])
    decay_full = jnp.dot(spread_t, jnp.broadcast_to(tot_col, (SSM_HEADS, SSM_STATE)),
                         preferred_element_type=F32, precision=HIGHEST)
    st_ref[...] = decay_full * st_ref[...] + jnp.concatenate(ds_groups, axis=0)
    z = z_ref[...]
    v = y * _silu(z)
    outs = []
    for g in range(SSM_GROUPS):
        vg = v[:, g * gw:(g + 1) * gw]
        outs.append(vg * lax.rsqrt(jnp.mean(vg * vg, axis=-1, keepdims=True) + RMS_EPS))
    y_ref[...] = jnp.concatenate(outs, axis=1) * gssm_ref[...]


def _ssd(xconv, dtx, proj, init_state, a_log, d_skip, g_ssm, seq_len):
    rows = xconv.shape[0]
    n_seq = rows // seq_len
    lc = min(SSM_CHUNK, seq_len)
    nc = seq_len // lc
    zblk = COL_Z // SSM_D_INNER
    vec = lambda a: pl.BlockSpec(a.shape, lambda b, c: (0, 0))
    return pl.pallas_call(
        functools.partial(_ssd_kernel, lc=lc),
        grid=(n_seq, nc),
        in_specs=[pl.BlockSpec((lc, CONV_DIM), lambda b, c: (b * nc + c, 0)),
                  pl.BlockSpec((lc, 128), lambda b, c: (b * nc + c, 0)),
                  pl.BlockSpec((lc, SSM_D_INNER), lambda b, c: (b * nc + c, zblk)),
                  pl.BlockSpec((None, SSM_D_INNER, SSM_STATE), lambda b, c: (b, 0, 0)),
                  vec(a_log), vec(d_skip), vec(g_ssm)],
        out_specs=[pl.BlockSpec((lc, SSM_D_INNER), lambda b, c: (b * nc + c, 0)),
                   pl.BlockSpec((None, SSM_D_INNER, SSM_STATE), lambda b, c: (b, 0, 0))],
        out_shape=[jax.ShapeDtypeStruct((rows, SSM_D_INNER), F32),
                   jax.ShapeDtypeStruct((n_seq, SSM_D_INNER, SSM_STATE), F32)],
        compiler_params=_params("parallel", "arbitrary"),
    )(xconv, dtx, proj, init_state, a_log, d_skip, g_ssm)


def _softmax_init(m_scr, l_scr, acc_scr):
    m_scr[...] = jnp.full_like(m_scr, MASK_VALUE)
    l_scr[...] = jnp.zeros_like(l_scr)
    acc_scr[...] = jnp.zeros_like(acc_scr)


def _softmax_step(s, m_scr, l_scr, acc_scr, pv):
    m_prev = m_scr[...]
    m_new = jnp.maximum(m_prev, jnp.max(s, axis=1, keepdims=True))
    alpha = jnp.exp(m_prev - m_new)
    p = jnp.exp(s - m_new)
    l_scr[...] = alpha * l_scr[...] + jnp.sum(p, axis=1, keepdims=True)
    acc_scr[...] = alpha * acc_scr[...] + pv(p.astype(BF16))
    m_scr[...] = m_new


def _mla_prompt_kernel(ql_ref, qr_ref, ckv_ref, kr_ref, wuv_ref, o_ref,
                       ckv_b, kr_b, m_scr, l_scr, acc_scr, *, tq, tk):
    i = pl.program_id(1)

    @pl.when(i == 0)
    def _():
        ckv_b[...] = ckv_ref[...].astype(BF16)
        kr_b[...] = kr_ref[...].astype(BF16)

    rows = MLA_HEADS * tq
    q1 = ql_ref[...].reshape(rows, MLA_KV_RANK).astype(BF16)
    q2 = qr_ref[...].reshape(rows, MLA_ROPE_DIM).astype(BF16)
    _softmax_init(m_scr, l_scr, acc_scr)
    q_pos = i * tq + lax.broadcasted_iota(jnp.int32, (rows, tk), 0) % tq
    key_off = lax.broadcasted_iota(jnp.int32, (rows, tk), 1)

    def body(j, carry):
        ks = pl.multiple_of(j * tk, tk)
        kc = ckv_b[pl.ds(ks, tk), :]
        s = (_dot_nt(q1, kc) + _dot_nt(q2, kr_b[pl.ds(ks, tk), :])) * MLA_SCALE
        s = jnp.where(key_off + ks <= q_pos, s, MASK_VALUE)
        _softmax_step(s, m_scr, l_scr, acc_scr, lambda p: _dot(p, kc))
        return carry

    lax.fori_loop(0, (i * tq) // tk + 1, body, 0)
    lat = (acc_scr[...] / l_scr[...]).astype(BF16)
    for h in range(MLA_HEADS):
        o_ref[:, h * MLA_V_DIM:(h + 1) * MLA_V_DIM] = _dot(lat[h * tq:(h + 1) * tq], wuv_ref[h])


def _mla_prompt(q_lat, q_rope, ckv, krope, w_uv_b, n_seq, seq_len):
    tq = 128
    tk = min(512, seq_len)
    nt = seq_len // tq
    rows = n_seq * seq_len
    return pl.pallas_call(
        functools.partial(_mla_prompt_kernel, tq=tq, tk=tk),
        grid=(n_seq, nt),
        in_specs=[pl.BlockSpec((MLA_HEADS, tq, MLA_KV_RANK), lambda b, i: (0, b * nt + i, 0)),
                  pl.BlockSpec((MLA_HEADS, tq, MLA_ROPE_DIM), lambda b, i: (0, b * nt + i, 0)),
                  pl.BlockSpec((seq_len, MLA_KV_RANK), lambda b, i: (b, 0)),
                  pl.BlockSpec((seq_len, MLA_ROPE_DIM), lambda b, i: (b, 0)),
                  pl.BlockSpec(w_uv_b.shape, lambda b, i: (0, 0, 0))],
        out_specs=pl.BlockSpec((tq, MLA_HEADS * MLA_V_DIM), lambda b, i: (b * nt + i, 0)),
        out_shape=jax.ShapeDtypeStruct((rows, MLA_HEADS * MLA_V_DIM), F32),
        scratch_shapes=[pltpu.VMEM((seq_len, MLA_KV_RANK), BF16), pltpu.VMEM((seq_len, MLA_ROPE_DIM), BF16),
                        pltpu.VMEM((MLA_HEADS * tq, 1), F32), pltpu.VMEM((MLA_HEADS * tq, 1), F32),
                        pltpu.VMEM((MLA_HEADS * tq, MLA_KV_RANK), F32)],
        compiler_params=_params("parallel", "arbitrary"),
    )(q_lat, q_rope, ckv, krope, w_uv_b)


def _moba_select(gate, n_valid, block_axis):
    nb = gate.shape[block_axis]
    blk = lax.broadcasted_iota(jnp.int32, gate.shape, block_axis)
    valid = blk < n_valid
    g = jnp.where(valid, gate, -jnp.inf)
    rank = jnp.zeros(gate.shape, jnp.int32)
    for m in range(nb):
        gm = lax.slice_in_dim(g, m, m + 1, axis=block_axis)
        ahead = (gm > g) | ((gm == g) & (m < blk))
        rank = rank + jnp.where(ahead, 1, 0)
    return jnp.where(valid & (rank < MOBA_TOPK), 1.0, 0.0).astype(F32)


def _moba_prompt_kernel(q_ref, k_ref, v_ref, o_ref, k_b, v_b, kmean_scr, m_scr, l_scr, acc_scr, *, nb):
    i = pl.program_id(1)
    tq = MOBA_BLOCK
    rows = MOBA_HEADS * tq

    @pl.when(i == 0)
    def _():
        k = k_ref[...]
        k_b[...] = k.astype(BF16)
        v_b[...] = v_ref[...].astype(BF16)
        kmean_scr[...] = jnp.zeros_like(kmean_scr)
        kmean_scr[0:nb, :] = jnp.mean(k.reshape(nb, MOBA_BLOCK, MOBA_HEAD_DIM), axis=1)

    q = q_ref[...].reshape(rows, MOBA_HEAD_DIM)
    qb = q.astype(BF16)
    gate_t = _dot_nt(kmean_scr[...], q, precision=HIGHEST)
    sel_t = _moba_select(gate_t[0:nb], i, 0)
    sel = jnp.concatenate([sel_t, jnp.zeros((128 - nb, rows), F32)], axis=0).T.astype(BF16)
    blk_row = lax.broadcasted_iota(jnp.int32, (128, MOBA_BLOCK), 0)
    _softmax_init(m_scr, l_scr, acc_scr)

    def attend(j, mask_fn):
        ks = pl.multiple_of(j * MOBA_BLOCK, MOBA_BLOCK)
        s = _dot_nt(qb, k_b[pl.ds(ks, MOBA_BLOCK), :]) * MOBA_SCALE
        s = jnp.where(mask_fn(), s, MASK_VALUE)
        _softmax_step(s, m_scr, l_scr, acc_scr, lambda p: _dot(p, v_b[pl.ds(ks, MOBA_BLOCK), :]))

    t_q = lax.broadcasted_iota(jnp.int32, (rows, MOBA_BLOCK), 0) % tq
    t_k = lax.broadcasted_iota(jnp.int32, (rows, MOBA_BLOCK), 1)
    attend(i, lambda: t_k <= t_q)

    def body(j, carry):
        onehot = jnp.where(blk_row == j, 1.0, 0.0).astype(BF16)
        attend(j, lambda: _dot(sel, onehot) > 0.5)
        return carry

    lax.fori_loop(0, i, body, 0)
    out = acc_scr[...] / l_scr[...]
    for h in range(MOBA_HEADS):
        o_ref[:, h * MOBA_HEAD_DIM:(h + 1) * MOBA_HEAD_DIM] = out[h * tq:(h + 1) * tq]


def _moba_prompt(qm, km, proj, n_seq, seq_len):
    nb = seq_len // MOBA_BLOCK
    rows = n_seq * seq_len
    vblk = COL_VM // MOBA_HEAD_DIM
    return pl.pallas_call(
        functools.partial(_moba_prompt_kernel, nb=nb),
        grid=(n_seq, nb),
        in_specs=[pl.BlockSpec((MOBA_HEADS, MOBA_BLOCK, MOBA_HEAD_DIM), lambda b, i: (0, b * nb + i, 0)),
                  pl.BlockSpec((seq_len, MOBA_HEAD_DIM), lambda b, i: (b, 0)),
                  pl.BlockSpec((seq_len, MOBA_HEAD_DIM), lambda b, i: (b, vblk))],
        out_specs=pl.BlockSpec((MOBA_BLOCK, MOBA_HEADS * MOBA_HEAD_DIM), lambda b, i: (b * nb + i, 0)),
        out_shape=jax.ShapeDtypeStruct((rows, MOBA_HEADS * MOBA_HEAD_DIM), F32),
        scratch_shapes=[pltpu.VMEM((seq_len, MOBA_HEAD_DIM), BF16), pltpu.VMEM((seq_len, MOBA_HEAD_DIM), BF16),
                        pltpu.VMEM((128, MOBA_HEAD_DIM), F32),
                        pltpu.VMEM((MOBA_HEADS * MOBA_BLOCK, 1), F32), pltpu.VMEM((MOBA_HEADS * MOBA_BLOCK, 1), F32),
                        pltpu.VMEM((MOBA_HEADS * MOBA_BLOCK, MOBA_HEAD_DIM), F32)],
        compiler_params=_params("parallel", "arbitrary"),
    )(qm, km, proj)


SAMPLE_CHUNKS = 4


def _attn_sample_kernel(pt_ref, ckv_pool, kr_pool, mk_pool, mv_pool,
                        ql_ref, qr_ref, qm_ref, cnew_ref, knew_ref, kmnew_ref, vmnew_ref, wuv_ref,
                        mla_o_ref, moba_o_ref,
                        ckv_buf, kr_buf, mk_buf, mv_buf, kstash, kmean_scr, selmask,
                        m1, l1, acc1, m2, l2, acc2, sems, *, layer, t_new, ch):
    seq = pl.program_id(0)
    n_seq = pl.num_programs(0)
    rows = MLA_HEADS * t_new
    ckeys = ch * PAGE_SIZE
    nb_chunk = ckeys // MOBA_BLOCK

    def copies_a(page_of, slot):
        out = []
        for j in range(ch):
            page, win = page_of(j), pl.ds(j * PAGE_SIZE, PAGE_SIZE)
            out.append(pltpu.make_async_copy(ckv_pool.at[layer, page], ckv_buf.at[slot, win, :], sems.at[0, slot]))
            out.append(pltpu.make_async_copy(kr_pool.at[layer, page], kr_buf.at[slot, :, win], sems.at[1, slot]))
            out.append(pltpu.make_async_copy(mk_pool.at[layer, page], mk_buf.at[slot, win, :], sems.at[2, slot]))
        return out

    def copies_b(page_of, slot):
        return [pltpu.make_async_copy(mv_pool.at[layer, page_of(j)],
                                      mv_buf.at[slot, pl.ds(j * PAGE_SIZE, PAGE_SIZE), :], sems.at[3, slot])
                for j in range(ch)]

    def start(copies, sq, k):
        for c in copies(lambda j: pt_ref[sq, k * ch + j], k % 2):
            c.start()

    def wait(copies, k):
        for c in copies(lambda j: 0, k % 2):
            c.wait()

    def refill(copies, k):
        if k + 2 < SAMPLE_CHUNKS:
            start(copies, seq, k + 2)
        else:
            @pl.when(seq + 1 < n_seq)
            def _():
                start(copies, seq + 1, k + 2 - SAMPLE_CHUNKS)

    @pl.when(seq == 0)
    def _():
        for k in range(2):
            start(copies_a, seq, k)
            start(copies_b, seq, k)

    t_q = lax.broadcasted_iota(jnp.int32, (rows, t_new), 0) % t_new
    t_k = lax.broadcasted_iota(jnp.int32, (rows, t_new), 1)
    causal_new = t_k <= t_q

    q1 = ql_ref[...].reshape(rows, MLA_KV_RANK).astype(BF16)
    q2 = qr_ref[...].reshape(rows, MLA_ROPE_DIM).astype(BF16)
    _softmax_init(m1, l1, acc1)
    for k in range(SAMPLE_CHUNKS):
        slot = k % 2
        wait(copies_a, k)
        kc = ckv_buf[slot].astype(BF16)
        s = (_dot_nt(q1, kc) + _dot(q2, kr_buf[slot].astype(BF16))) * MLA_SCALE
        _softmax_step(s, m1, l1, acc1, lambda p: _dot(p, kc))
        mk = mk_buf[slot]
        kstash[k * ckeys:(k + 1) * ckeys, :] = mk.astype(BF16)
        kmean_scr[k * nb_chunk:(k + 1) * nb_chunk, :] = jnp.mean(
            mk.reshape(nb_chunk, MOBA_BLOCK, MOBA_HEAD_DIM), axis=1)
        refill(copies_a, k)

    cn = cnew_ref[...].astype(BF16)
    sn = (_dot_nt(q1, cn) + _dot_nt(q2, knew_ref[...].astype(BF16))) * MLA_SCALE
    _softmax_step(jnp.where(causal_new, sn, MASK_VALUE), m1, l1, acc1, lambda p: _dot(p, cn))
    lat = (acc1[...] / l1[...]).astype(BF16)
    for h in range(MLA_HEADS):
        mla_o_ref[:, h * MLA_V_DIM:(h + 1) * MLA_V_DIM] = _dot(lat[h * t_new:(h + 1) * t_new], wuv_ref[h])

    q = qm_ref[...].reshape(rows, MOBA_HEAD_DIM)
    qb = q.astype(BF16)
    _softmax_init(m2, l2, acc2)
    vn = vmnew_ref[...].astype(BF16)
    sm = _dot_nt(qb, kmnew_ref[...].astype(BF16)) * MOBA_SCALE
    _softmax_step(jnp.where(causal_new, sm, MASK_VALUE), m2, l2, acc2, lambda p: _dot(p, vn))
    nb_past = SAMPLE_CHUNKS * nb_chunk
    gate = _dot_nt(q, kmean_scr[...], precision=HIGHEST)
    sel = _moba_select(gate, nb_past, 1)
    for b in range(nb_past):
        selmask[:, b * MOBA_BLOCK:(b + 1) * MOBA_BLOCK] = jnp.broadcast_to(sel[:, b:b + 1], (rows, MOBA_BLOCK))

    for k in range(SAMPLE_CHUNKS):
        slot = k % 2
        wait(copies_b, k)
        v = mv_buf[slot].astype(BF16)
        s = _dot_nt(qb, kstash[k * ckeys:(k + 1) * ckeys, :]) * MOBA_SCALE
        s = jnp.where(selmask[:, k * ckeys:(k + 1) * ckeys] > 0.5, s, MASK_VALUE)
        _softmax_step(s, m2, l2, acc2, lambda p: _dot(p, v))
        refill(copies_b, k)

    out = acc2[...] / l2[...]
    for h in range(MOBA_HEADS):
        moba_o_ref[:, h * MOBA_HEAD_DIM:(h + 1) * MOBA_HEAD_DIM] = out[h * t_new:(h + 1) * t_new]


def _attn_sample(page_table, ckv_pool, kr_pool_t, mk_pool, mv_pool, layer,
                 q_lat, q_rope, qm, ckv_new, kr_new, km_new, proj, w_uv_b, t_new):
    n_seq, n_pages = page_table.shape
    assert n_pages % (SAMPLE_CHUNKS * MOBA_BLOCK // PAGE_SIZE) == 0
    ch = n_pages // SAMPLE_CHUNKS
    ckeys = ch * PAGE_SIZE
    past_len = n_pages * PAGE_SIZE
    rows = MLA_HEADS * t_new
    vblk = COL_VM // MOBA_HEAD_DIM
    hbm = pl.BlockSpec(memory_space=pl.ANY)
    heads = lambda w: pl.BlockSpec((MLA_HEADS, t_new, w), lambda s, pt: (0, s, 0))
    new_rows = lambda w, blk=0: pl.BlockSpec((t_new, w), lambda s, pt: (s, blk))
    grid_spec = pltpu.PrefetchScalarGridSpec(
        num_scalar_prefetch=1, grid=(n_seq,),
        in_specs=[hbm, hbm, hbm, hbm,
                  heads(MLA_KV_RANK), heads(MLA_ROPE_DIM), heads(MOBA_HEAD_DIM),
                  new_rows(MLA_KV_RANK), new_rows(MLA_ROPE_DIM), new_rows(MOBA_HEAD_DIM),
                  new_rows(MOBA_HEAD_DIM, vblk),
                  pl.BlockSpec(w_uv_b.shape, lambda s, pt: (0, 0, 0))],
        out_specs=[new_rows(MLA_HEADS * MLA_V_DIM), new_rows(MOBA_HEADS * MOBA_HEAD_DIM)],
        scratch_shapes=[pltpu.VMEM((2, ckeys, MLA_KV_RANK), F32),
                        pltpu.VMEM((2, MLA_ROPE_DIM, ckeys), F32),
                        pltpu.VMEM((2, ckeys, MOBA_HEAD_DIM), F32),
                        pltpu.VMEM((2, ckeys, MOBA_HEAD_DIM), F32),
                        pltpu.VMEM((past_len, MOBA_HEAD_DIM), BF16),
                        pltpu.VMEM((past_len // MOBA_BLOCK, MOBA_HEAD_DIM), F32),
                        pltpu.VMEM((rows, past_len), F32),
                        pltpu.VMEM((rows, 1), F32), pltpu.VMEM((rows, 1), F32),
                        pltpu.VMEM((rows, MLA_KV_RANK), F32),
                        pltpu.VMEM((rows, 1), F32), pltpu.VMEM((rows, 1), F32),
                        pltpu.VMEM((rows, MOBA_HEAD_DIM), F32),
                        pltpu.SemaphoreType.DMA((4, 2))])
    return pl.pallas_call(
        functools.partial(_attn_sample_kernel, layer=layer, t_new=t_new, ch=ch),
        grid_spec=grid_spec,
        out_shape=[jax.ShapeDtypeStruct((n_seq * t_new, MLA_HEADS * MLA_V_DIM), F32),
                   jax.ShapeDtypeStruct((n_seq * t_new, MOBA_HEADS * MOBA_HEAD_DIM), F32)],
        compiler_params=_params("arbitrary"),
    )(page_table, ckv_pool, kr_pool_t, mk_pool, mv_pool, q_lat, q_rope, qm, ckv_new, kr_new, km_new, proj, w_uv_b)


def _outproj_kernel(mla_ref, moba_ref, ssm_ref, w_ref, x_ref, gt_ref, lg_ref, lb_ref, o_ref, *, alpha):
    w_mla = MLA_HEADS * MLA_V_DIM
    w_moba = MOBA_HEADS * MOBA_HEAD_DIM
    mixed = (_dot(mla_ref[...].astype(BF16), w_ref[0:w_mla, :])
             + _dot(moba_ref[...].astype(BF16), w_ref[w_mla:w_mla + w_moba, :])
             + _dot(ssm_ref[...].astype(BF16), w_ref[w_mla + w_moba:, :]))
    v = alpha * x_ref[...] + gt_ref[...] * mixed
    o_ref[...] = _layer_norm_rows(v, lg_ref[...], lb_ref[...])


def _outproj(mla_out, moba_out, ssm_out, w_out_b, layer, x, mod, ln_g, ln_b, alpha, tm):
    rows = x.shape[0]
    row = lambda w: pl.BlockSpec((tm, w), lambda i: (i, 0))
    vec_spec = pl.BlockSpec((1, D_MODEL), lambda i: (0, 0))
    return pl.pallas_call(
        functools.partial(_outproj_kernel, alpha=alpha),
        grid=(rows // tm,),
        in_specs=[row(mla_out.shape[1]), row(moba_out.shape[1]), row(ssm_out.shape[1]),
                  pl.BlockSpec((None, D_MODEL, D_MODEL), lambda i: (layer, 0, 0)),
                  row(D_MODEL), mod.spec(tm), vec_spec, vec_spec],
        out_specs=row(D_MODEL),
        out_shape=jax.ShapeDtypeStruct((rows, D_MODEL), F32),
        compiler_params=_params("parallel"),
    )(mla_out, moba_out, ssm_out, w_out_b, x, mod.arrays[2], ln_g, ln_b)


def _trunk(x, mods_by_layer, seq_len, tabs, conv_init, ssm_init, attn, w, tm):
    depth = w["w_in_p"].shape[0]
    alpha = (2 * depth) ** 0.25
    n_seq = x.shape[0] // seq_len
    tab_tiles = tabs[0].shape[0] // tm
    states = []
    for l in range(depth):
        mods = mods_by_layer[l]
        ln_g, ln_b = w["ln_g"][l], w["ln_b"][l]
        x = _ffn(x, mods[0], w["wg"], w["wu"], w["wd"], ln_g[0:1], ln_b[0:1], l, 0, alpha, tm)
        proj = _inproj(x, mods[1], w["w_in_p"], l, tm)
        q_lat, q_rope, ckv, krope, qm, km, dtx = _prep(
            proj, tabs, tab_tiles, w["g_q"][l:l + 1], w["w_uq_p"][l], w["g_kv"][l:l + 1], w["w_uk_t"][l],
            w["dt_bias_p"][l:l + 1], tm)
        conv_tm = min(tm, seq_len)
        init8 = jnp.pad(conv_init[l], ((0, 0), (8 - (CONV_WIDTH - 1), 0), (0, 0)))
        xconv = _conv(proj, init8, w["conv_w"][l], w["conv_b"][l:l + 1], seq_len, conv_tm)
        ssm_out, ssm_state = _ssd(xconv, dtx, proj, ssm_init[l], w["a_log"][l:l + 1], w["d_skip"][l:l + 1],
                                  w["g_ssm"][l:l + 1], seq_len)
        mla_out, moba_out = attn(l, q_lat, q_rope, ckv, krope, qm, km, proj)
        x = _outproj(mla_out, moba_out, ssm_out, w["w_out_b"], l, x, mods[1], ln_g[1:2], ln_b[1:2], alpha, tm)
        x = _ffn(x, mods[2], w["wg"], w["wu"], w["wd"], ln_g[2:3], ln_b[2:3], l, 1, alpha, tm)
        vm = proj[:, COL_VM:COL_VM + MOBA_HEAD_DIM]
        xbc = proj[:, COL_XBC:COL_XBC + CONV_DIM].reshape(n_seq, seq_len, CONV_DIM)
        conv_state = jnp.concatenate([conv_init[l], xbc], axis=1)[:, -(CONV_WIDTH - 1):]
        states.append((ckv.reshape(n_seq, seq_len, MLA_KV_RANK),
                       krope.reshape(n_seq, seq_len, MLA_ROPE_DIM),
                       km.reshape(n_seq, seq_len, 1, MOBA_HEAD_DIM),
                       vm.reshape(n_seq, seq_len, 1, MOBA_HEAD_DIM),
                       ssm_state.reshape(n_seq, SSM_HEADS, SSM_HEAD_DIM, SSM_STATE),
                       conv_state))
    return x, [jnp.stack([s[i] for s in states]) for i in range(len(states[0]))]


def _row_tile(rows, seq_len, largest):
    for tm in (512, 256, 128, 64, 32, 16, 8):
        if tm <= largest and rows % tm == 0 and (seq_len % tm == 0 or tm % seq_len == 0):
            return tm
    raise ValueError("row count must be a multiple of 8")


def kernel(x_prompt, x_sample, cache_mla_ckv, cache_mla_krope, cache_moba_k, cache_moba_v, state_ssm, state_conv,
           page_table, c_prompt, c_sample, w_ada, b_ada, ln_g, ln_b, ffn_w_gate, ffn_w_up, ffn_w_down,
           w_in, g_q, w_uq, g_kv, w_uk, w_uv, conv_w, conv_b, dt_bias, a_log, d_skip, g_ssm, w_out):
    depth = w_in.shape[0]
    n_prompt, seq_len, _ = x_prompt.shape
    n_sample, t_new, _ = x_sample.shape
    n_pages = page_table.shape[1]
    past_len = n_pages * PAGE_SIZE
    n_pool = cache_mla_ckv.shape[1]

    assert past_len % MOBA_BLOCK == 0 and t_new <= MOBA_BLOCK and seq_len % MOBA_BLOCK == 0
    names = ["cq", "ckv", "kr", "qm", "km", "vm", "z", "xbc", "dt"]
    widths = [MLA_Q_RANK, MLA_KV_RANK, MLA_ROPE_DIM, MOBA_HEADS * MOBA_HEAD_DIM, MOBA_HEAD_DIM, MOBA_HEAD_DIM,
              SSM_D_INNER, CONV_DIM, SSM_HEADS]
    pieces, start = {}, 0
    for name, width in zip(names, widths):
        pieces[name] = (start, start + width)
        start += width
    order = ["cq", "qm", "ckv", "km", "vm", "xbc", "z", "kr", "dt"]
    w_in_p = jnp.concatenate([w_in[:, :, pieces[k][0]:pieces[k][1]] for k in order]
                             + [jnp.zeros((depth, D_MODEL, IN_COLS_PADDED - w_in.shape[2]), w_in.dtype)],
                             axis=2).astype(BF16)
    qk = MLA_NOPE_DIM + MLA_ROPE_DIM
    w_uq_h = w_uq.reshape(depth, MLA_Q_RANK, MLA_HEADS, qk)
    w_uq_p = jnp.concatenate([w_uq_h[..., :MLA_NOPE_DIM].reshape(depth, MLA_Q_RANK, MLA_HEADS * MLA_NOPE_DIM),
                              w_uq_h[..., MLA_NOPE_DIM:].reshape(depth, MLA_Q_RANK, MLA_HEADS * MLA_ROPE_DIM)],
                             axis=2).astype(BF16)
    dt_bias_p = jnp.zeros((depth, 128), F32).at[:, DT_LANE:DT_LANE + SSM_HEADS].set(dt_bias)
    w = dict(
        wg=ffn_w_gate.astype(BF16), wu=ffn_w_up.astype(BF16), wd=ffn_w_down.astype(BF16),
        w_in_p=w_in_p, w_uq_p=w_uq_p, w_uk_t=jnp.swapaxes(w_uk, 2, 3).astype(BF16), w_uv_b=w_uv.astype(BF16),
        w_out_b=w_out.astype(BF16), g_q=g_q, g_kv=g_kv, ln_g=ln_g, ln_b=ln_b,
        conv_w=conv_w.reshape(depth, CONV_WIDTH, CONV_DIM), conv_b=conv_b, dt_bias_p=dt_bias_p,
        a_log=a_log, d_skip=d_skip, g_ssm=g_ssm)

    n_c = n_prompt + n_sample
    c_rows = -(-n_c // 8) * 8
    c_all = jnp.concatenate([c_prompt, c_sample, jnp.zeros((c_rows - n_c, D_MODEL), F32)], axis=0)
    mod = _ada(c_all, w_ada, b_ada)
    mods_p = [_make_mods(mod[l, :n_prompt], seq_len, per_token=False) for l in range(depth)]
    mods_s = [_make_mods(mod[l, n_prompt:n_c], t_new, per_token=True) for l in range(depth)]

    tabs_p = _rope_tables(jnp.arange(seq_len, dtype=jnp.int32))
    conv0 = jnp.zeros((depth, n_prompt, CONV_WIDTH - 1, CONV_DIM), F32)
    ssm0 = jnp.zeros((depth, n_prompt, SSM_D_INNER, SSM_STATE), F32)

    def attn_prompt(l, q_lat, q_rope, ckv, krope, qm, km, proj):
        return (_mla_prompt(q_lat, q_rope, ckv, krope, w["w_uv_b"][l], n_prompt, seq_len),
                _moba_prompt(qm, km, proj, n_prompt, seq_len))

    tm_p = _row_tile(n_prompt * seq_len, seq_len, 512)
    y_p, st_p = _trunk(x_prompt.reshape(n_prompt * seq_len, D_MODEL), mods_p, seq_len, tabs_p, conv0, ssm0,
                       attn_prompt, w, tm_p)

    pos_s = past_len + jnp.arange(t_new, dtype=jnp.int32)
    tabs_s = tuple(jnp.tile(t, (n_sample, 1)) for t in _rope_tables(pos_s))
    mk_pool = cache_moba_k.reshape(depth, n_pool, PAGE_SIZE, MOBA_HEAD_DIM)
    mv_pool = cache_moba_v.reshape(depth, n_pool, PAGE_SIZE, MOBA_HEAD_DIM)

    kr_pool_t = jnp.swapaxes(cache_mla_krope, 2, 3)

    def attn_sample(l, q_lat, q_rope, ckv, krope, qm, km, proj):
        return _attn_sample(page_table, cache_mla_ckv, kr_pool_t, mk_pool, mv_pool, l,
                            q_lat, q_rope, qm, ckv, krope, km, proj, w["w_uv_b"][l], t_new)

    tm_s = _row_tile(n_sample * t_new, t_new, 256)
    ssm_init_s = state_ssm.reshape(depth, n_sample, SSM_D_INNER, SSM_STATE)
    y_s, st_s = _trunk(x_sample.reshape(n_sample * t_new, D_MODEL), mods_s, t_new, tabs_s, state_conv, ssm_init_s,
                       attn_sample, w, tm_s)

    return (y_p.reshape(n_prompt, seq_len, D_MODEL), y_s.reshape(n_sample, t_new, D_MODEL),
            *st_p, *st_s)
```

```python
import functools
import math

import jax
import jax.numpy as jnp
from jax import lax
from jax.experimental import pallas as pl
from jax.experimental.pallas import tpu as pltpu

F32 = jnp.float32
BF16 = jnp.bfloat16

D_MODEL = 2048
PAGE_SIZE = 128
MLA_HEADS = 4
MLA_Q_RANK = 512
MLA_KV_RANK = 256
MLA_NOPE_DIM = 128
MLA_ROPE_DIM = 64
MLA_V_DIM = 128
MLA_SCALE = (MLA_NOPE_DIM + MLA_ROPE_DIM) ** -0.5
MOBA_HEADS = 4
MOBA_HEAD_DIM = 128
MOBA_BLOCK = 256
MOBA_TOPK = 3
MOBA_SCALE = MOBA_HEAD_DIM ** -0.5
SSM_HEADS = 16
SSM_HEAD_DIM = 64
SSM_D_INNER = SSM_HEADS * SSM_HEAD_DIM
SSM_GROUPS = 2
SSM_STATE = 128
SSM_CHUNK = 128
CONV_WIDTH = 4
CONV_DIM = SSM_D_INNER + 2 * SSM_GROUPS * SSM_STATE
MACARON_WEIGHT = 0.5
ROPE_THETA = 10000.0
LN_EPS = 1e-5
RMS_EPS = 1e-6

COL_CQ = 0
COL_QM = 512
COL_CKV = 1024
COL_KM = 1280
COL_VM = 1408
COL_XBC = 1536
COL_Z = 3072
COL_KRDT = 4096
IN_COLS_PADDED = 4224
DT_LANE = 64

MASK_VALUE = -1e30
VMEM_LIMIT = 56 * 1024 * 1024

HIGHEST = lax.Precision.HIGHEST
NT_DIMS = (((1,), (1,)), ((), ()))
TN_DIMS = (((0,), (0,)), ((), ()))


def _params(*sem):
    return pltpu.CompilerParams(dimension_semantics=sem, vmem_limit_bytes=VMEM_LIMIT)


def _silu(v):
    return v * jax.nn.sigmoid(v)


def _dot(a, b):
    return jnp.dot(a, b, preferred_element_type=F32)


def _dot_nt(a, b, precision=None):
    return lax.dot_general(a, b, NT_DIMS, preferred_element_type=F32, precision=precision)


def _layer_norm_rows(v, g, b):
    mu = jnp.mean(v, axis=-1, keepdims=True)
    d = v - mu
    var = jnp.mean(d * d, axis=-1, keepdims=True)
    return d * lax.rsqrt(var + LN_EPS) * g + b


def _ada_kernel(c_ref, w_ref, b_ref, o_ref):
    h = _silu(c_ref[...]).astype(BF16)
    o_ref[...] = _dot(h, w_ref[...].astype(BF16)) + b_ref[...]


def _ada(c_all, w_ada, b_ada):
    depth, d, n = w_ada.shape
    rows = c_all.shape[0]
    tn = 1024
    return pl.pallas_call(
        _ada_kernel,
        grid=(depth, n // tn),
        in_specs=[
            pl.BlockSpec((rows, d), lambda l, j: (0, 0)),
            pl.BlockSpec((None, d, tn), lambda l, j: (l, 0, j)),
            pl.BlockSpec((None, 1, tn), lambda l, j: (l, 0, j)),
        ],
        out_specs=pl.BlockSpec((None, rows, tn), lambda l, j: (l, 0, j)),
        out_shape=jax.ShapeDtypeStruct((depth, rows, n), F32),
        compiler_params=_params("parallel", "parallel"),
    )(c_all, w_ada, b_ada.reshape(depth, 1, n))


class _Mod:
    def __init__(self, arrays, seq_len):
        self.arrays = arrays
        self.seq_len = seq_len

    def spec(self, tm):
        if self.seq_len >= tm:
            tps = self.seq_len // tm
            return pl.BlockSpec((None, 1, D_MODEL), lambda i, *_: (i // tps, 0, 0))
        return pl.BlockSpec((tm // self.seq_len, 1, D_MODEL), lambda i, *_: (i, 0, 0))


def _mod_rows(m_ref, tm):
    m = m_ref[...]
    if m.ndim == 2:
        return m
    g = m.shape[0]
    return jnp.broadcast_to(m, (g, tm // g, D_MODEL)).reshape(tm, D_MODEL)


def _make_mods(mod, seq_len):
    n_seq = mod.shape[0]
    m = mod.reshape(n_seq, 3, 3, 1, D_MODEL)
    return [_Mod([m[:, j, k] for k in range(3)], seq_len) for j in range(3)]


def _ffn_kernel(x_ref, sh_ref, sc_ref, gt_ref, wg_ref, wu_ref, wd_ref, lg_ref, lb_ref, o_ref,
                h_scr, acc_scr, *, alpha, weight):
    j = pl.program_id(1)
    tm = x_ref.shape[0]

    @pl.when(j == 0)
    def _():
        h_scr[...] = (x_ref[...] * (1.0 + _mod_rows(sc_ref, tm)) + _mod_rows(sh_ref, tm)).astype(BF16)
        acc_scr[...] = jnp.zeros_like(acc_scr)

    h = h_scr[...]
    a = _dot(h, wg_ref[...])
    u = _dot(h, wu_ref[...])
    act = (_silu(a) * u).astype(BF16)
    acc_scr[...] += _dot(act, wd_ref[...])

    @pl.when(j == pl.num_programs(1) - 1)
    def _():
        v = alpha * x_ref[...] + weight * _mod_rows(gt_ref, tm) * acc_scr[...]
        o_ref[...] = _layer_norm_rows(v, lg_ref[...], lb_ref[...])


def _ffn(x, mod, wg, wu, wd, ln_g, ln_b, layer, slot, alpha, tm):
    rows = x.shape[0]
    d_ff = wg.shape[-1]
    tf = 512 if d_ff % 512 == 0 else d_ff
    row_spec = pl.BlockSpec((tm, D_MODEL), lambda i, j: (i, 0))
    vec_spec = pl.BlockSpec((1, D_MODEL), lambda i, j: (0, 0))
    return pl.pallas_call(
        functools.partial(_ffn_kernel, alpha=alpha, weight=MACARON_WEIGHT),
        grid=(rows // tm, d_ff // tf),
        in_specs=[row_spec, mod.spec(tm), mod.spec(tm), mod.spec(tm),
                  pl.BlockSpec((None, None, D_MODEL, tf), lambda i, j: (layer, slot, 0, j)),
                  pl.BlockSpec((None, None, D_MODEL, tf), lambda i, j: (layer, slot, 0, j)),
                  pl.BlockSpec((None, None, tf, D_MODEL), lambda i, j: (layer, slot, j, 0)),
                  vec_spec, vec_spec],
        out_specs=row_spec,
        out_shape=jax.ShapeDtypeStruct((rows, D_MODEL), F32),
        scratch_shapes=[pltpu.VMEM((tm, D_MODEL), BF16), pltpu.VMEM((tm, D_MODEL), F32)],
        compiler_params=_params("parallel", "arbitrary"),
    )(x, mod.arrays[0], mod.arrays[1], mod.arrays[2], wg, wu, wd, ln_g, ln_b)


def _inproj_kernel(x_ref, sh_ref, sc_ref, w_ref, o_ref):
    tm = x_ref.shape[0]
    h = (x_ref[...] * (1.0 + _mod_rows(sc_ref, tm)) + _mod_rows(sh_ref, tm)).astype(BF16)
    o_ref[...] = _dot(h, w_ref[...])


def _inproj(x, mod, w_in_p, layer, tm):
    rows = x.shape[0]
    tn = IN_COLS_PADDED // 3
    return pl.pallas_call(
        _inproj_kernel,
        grid=(rows // tm, 3),
        in_specs=[pl.BlockSpec((tm, D_MODEL), lambda i, j: (i, 0)), mod.spec(tm), mod.spec(tm),
                  pl.BlockSpec((None, D_MODEL, tn), lambda i, j: (layer, 0, j))],
        out_specs=pl.BlockSpec((tm, tn), lambda i, j: (i, j)),
        out_shape=jax.ShapeDtypeStruct((rows, IN_COLS_PADDED), F32),
        compiler_params=_params("parallel", "arbitrary"),
    )(x, mod.arrays[0], mod.arrays[1], w_in_p)


def _rope64(v, cos, sin):
    lane = lax.broadcasted_iota(jnp.int32, v.shape, 1)
    swap = jnp.where(lane % 64 < 32, pltpu.roll(v, 96, axis=1), pltpu.roll(v, 32, axis=1))
    return v * cos + swap * sin


def _rope128(v, cos, sin):
    return v * cos + pltpu.roll(v, 64, axis=1) * sin


def _rms(v, g):
    return v * lax.rsqrt(jnp.mean(v * v, axis=-1, keepdims=True) + RMS_EPS) * g


def _prep_kernel(cq_ref, qm_ref, ckvr_ref, km_ref, krdt_ref, cosa_ref, sina_ref, cosb_ref, sinb_ref,
                 gq_ref, wuq_ref, gkv_ref, wuk_ref, dtb_ref,
                 qlat_ref, qrope_ref, ckv_ref, krope_ref, qmo_ref, kmo_ref, dtx_ref):
    cosa, sina = cosa_ref[...], sina_ref[...]
    cosb, sinb = cosb_ref[...], sinb_ref[...]
    cqn = _rms(cq_ref[...], gq_ref[...]).astype(BF16)
    q = _dot(cqn, wuq_ref[...])
    nope_w = MLA_HEADS * MLA_NOPE_DIM
    for pair in range(MLA_HEADS // 2):
        r = _rope64(q[:, nope_w + pair * 128: nope_w + (pair + 1) * 128], cosa, sina)
        qrope_ref[2 * pair] = r[:, :MLA_ROPE_DIM]
        qrope_ref[2 * pair + 1] = r[:, MLA_ROPE_DIM:]
    for h in range(MLA_HEADS):
        qn = q[:, h * MLA_NOPE_DIM:(h + 1) * MLA_NOPE_DIM].astype(BF16)
        qlat_ref[h] = _dot(qn, wuk_ref[h])
    ckv_ref[...] = _rms(ckvr_ref[...], gkv_ref[...])
    krdt = krdt_ref[...]
    krope_ref[...] = _rope64(krdt, cosa, sina)[:, :MLA_ROPE_DIM]
    dtx_ref[...] = jax.nn.softplus(krdt + dtb_ref[...])
    qm = qm_ref[...]
    for h in range(MOBA_HEADS):
        qmo_ref[h] = _rope128(qm[:, h * MOBA_HEAD_DIM:(h + 1) * MOBA_HEAD_DIM], cosb, sinb)
    kmo_ref[...] = _rope128(km_ref[...], cosb, sinb)


def _prep(proj, tabs, tab_tiles, g_q, w_uq_p, g_kv, w_uk_t, dt_bias_p, tm):
    rows = proj.shape[0]

    def col(width, start):
        return pl.BlockSpec((tm, width), lambda i: (i, start // width))

    tab_spec = pl.BlockSpec((tm, 128), lambda i: (i % tab_tiles, 0))

    def full(a):
        return pl.BlockSpec(a.shape, lambda i: (0,) * a.ndim)

    heads_spec = lambda w: pl.BlockSpec((MLA_HEADS, tm, w), lambda i: (0, i, 0))
    row_spec = lambda w: pl.BlockSpec((tm, w), lambda i: (i, 0))
    return pl.pallas_call(
        _prep_kernel,
        grid=(rows // tm,),
        in_specs=[col(512, COL_CQ), col(512, COL_QM), col(256, COL_CKV), col(128, COL_KM), col(128, COL_KRDT),
                  tab_spec, tab_spec, tab_spec, tab_spec,
                  full(g_q), full(w_uq_p), full(g_kv), full(w_uk_t), full(dt_bias_p)],
        out_specs=[heads_spec(MLA_KV_RANK), heads_spec(MLA_ROPE_DIM), row_spec(MLA_KV_RANK), row_spec(MLA_ROPE_DIM),
                   heads_spec(MOBA_HEAD_DIM), row_spec(MOBA_HEAD_DIM), row_spec(128)],
        out_shape=[jax.ShapeDtypeStruct((MLA_HEADS, rows, MLA_KV_RANK), F32),
                   jax.ShapeDtypeStruct((MLA_HEADS, rows, MLA_ROPE_DIM), F32),
                   jax.ShapeDtypeStruct((rows, MLA_KV_RANK), F32),
                   jax.ShapeDtypeStruct((rows, MLA_ROPE_DIM), F32),
                   jax.ShapeDtypeStruct((MOBA_HEADS, rows, MOBA_HEAD_DIM), F32),
                   jax.ShapeDtypeStruct((rows, MOBA_HEAD_DIM), F32),
                   jax.ShapeDtypeStruct((rows, 128), F32)],
        compiler_params=_params("parallel"),
    )(proj, proj, proj, proj, proj, *tabs, g_q, w_uq_p, g_kv, w_uk_t, dt_bias_p)


def _rope_tables(pos):
    def tab(half):
        inv = ROPE_THETA ** (-jnp.arange(half, dtype=F32) / half)
        ang = pos.astype(F32)[:, None] * inv[None, :]
        return jnp.cos(ang), jnp.sin(ang)

    c32, s32 = tab(MLA_ROPE_DIM // 2)
    c64, s64 = tab(MOBA_HEAD_DIM // 2)
    cosa = jnp.concatenate([c32, c32, c32, c32], axis=1)
    sina = jnp.concatenate([-s32, s32, -s32, s32], axis=1)
    cosb = jnp.concatenate([c64, c64], axis=1)
    sinb = jnp.concatenate([-s64, s64], axis=1)
    return cosa, sina, cosb, sinb


def _conv_kernel(cur_ref, prev_ref, init_ref, w_ref, b_ref, o_ref, pad_scr, *, tiles_per_seq, tm):
    i = pl.program_id(0)
    at_start = (i % tiles_per_seq) == 0
    pad_scr[0:8, :] = jnp.where(at_start, init_ref[...], prev_ref[...])
    pad_scr[8:8 + tm, :] = cur_ref[...]
    acc = b_ref[...] + w_ref[CONV_WIDTH - 1:CONV_WIDTH, :] * cur_ref[...]
    for k in range(CONV_WIDTH - 1):
        shift = CONV_WIDTH - 1 - k
        acc = acc + w_ref[k:k + 1, :] * pad_scr[8 - shift:8 - shift + tm, :]
    o_ref[...] = _silu(acc)


def _conv(proj, init8, conv_w, conv_b, seq_len, tm):
    rows = proj.shape[0]
    tiles_per_seq = seq_len // tm
    blk = COL_XBC // CONV_DIM
    return pl.pallas_call(
        functools.partial(_conv_kernel, tiles_per_seq=tiles_per_seq, tm=tm),
        grid=(rows // tm,),
        in_specs=[pl.BlockSpec((tm, CONV_DIM), lambda i: (i, blk)),
                  pl.BlockSpec((8, CONV_DIM), lambda i: (jnp.maximum(i * (tm // 8) - 1, 0), blk)),
                  pl.BlockSpec((None, 8, CONV_DIM), lambda i: (i // tiles_per_seq, 0, 0)),
                  pl.BlockSpec((CONV_WIDTH, CONV_DIM), lambda i: (0, 0)),
                  pl.BlockSpec((1, CONV_DIM), lambda i: (0, 0))],
        out_specs=pl.BlockSpec((tm, CONV_DIM), lambda i: (i, 0)),
        out_shape=jax.ShapeDtypeStruct((rows, CONV_DIM), F32),
        scratch_shapes=[pltpu.VMEM((tm + 8, CONV_DIM), F32)],
        compiler_params=_params("arbitrary"),
    )(proj, proj, init8, conv_w, conv_b)


def _ssd_kernel(xc_ref, dtx_ref, z_ref, init_ref, alog_ref, dskip_ref, gssm_ref, y_ref, st_ref, *, lc):
    c = pl.program_id(1)

    @pl.when(c == 0)
    def _():
        st_ref[...] = init_ref[...]

    gw = SSM_D_INNER // SSM_GROUPS
    xs = xc_ref[:, 0:SSM_D_INNER]
    dt = dtx_ref[:, DT_LANE:DT_LANE + SSM_HEADS]
    a_head = -jnp.exp(alog_ref[...])
    adt = dt * a_head
    row = lax.broadcasted_iota(jnp.int32, (lc, lc), 0)
    colm = lax.broadcasted_iota(jnp.int32, (lc, lc), 1)
    tril = row >= colm
    a_cs = jnp.dot(tril.astype(F32), adt, preferred_element_type=F32, precision=HIGHEST)
    eye = (lax.broadcasted_iota(jnp.int32, (SSM_HEADS, SSM_HEADS), 0)
           == lax.broadcasted_iota(jnp.int32, (SSM_HEADS, SSM_HEADS), 1)).astype(F32)
    a_cs_t = _dot_nt(eye, a_cs, precision=HIGHEST)
    a_tot = a_cs[lc - 1:lc, :]
    spread = (lax.broadcasted_iota(jnp.int32, (SSM_HEADS, SSM_D_INNER), 1) // SSM_HEAD_DIM
              == lax.broadcasted_iota(jnp.int32, (SSM_HEADS, SSM_D_INNER), 0)).astype(F32)
    spread_t = (lax.broadcasted_iota(jnp.int32, (SSM_D_INNER, SSM_HEADS), 0) // SSM_HEAD_DIM
                == lax.broadcasted_iota(jnp.int32, (SSM_D_INNER, SSM_HEADS), 1)).astype(F32)

    def expand(v):
        return jnp.dot(v, spread, preferred_element_type=F32, precision=HIGHEST)

    xdt = xs * expand(dt)
    xd_state = (xdt * expand(jnp.exp(a_tot - a_cs))).astype(BF16)
    lane = lax.broadcasted_iota(jnp.int32, (lc, 128), 1)
    y_groups = []
    ds_groups = []
    for g in range(SSM_GROUPS):
        bg = xc_ref[:, SSM_D_INNER + g * SSM_STATE: SSM_D_INNER + (g + 1) * SSM_STATE].astype(BF16)
        cg = xc_ref[:, SSM_D_INNER + (SSM_GROUPS + g) * SSM_STATE:
                    SSM_D_INNER + (SSM_GROUPS + g + 1) * SSM_STATE].astype(BF16)
        cb = _dot_nt(cg, bg)
        s_g = st_ref[g * gw:(g + 1) * gw, :]
        y_off = _dot_nt(cg, s_g.astype(BF16))
        pairs = []
        for pr in range(gw // 128):
            x_pair = xdt[:, g * gw + pr * 128: g * gw + (pr + 1) * 128]
            acc = None
            for sub in range(2):
                h = (g * gw + pr * 128) // SSM_HEAD_DIM + sub
                seg = a_cs[:, h:h + 1] - a_cs_t[h:h + 1, :]
                decay = jnp.exp(jnp.where(tril, seg, -jnp.inf))
                gmat = (cb * decay).astype(BF16)
                x_h = jnp.where((lane // SSM_HEAD_DIM) == sub, x_pair, 0.0).astype(BF16)
                part = _dot(gmat, x_h)
                acc = part if acc is None else acc + part
            pairs.append(acc)
        y_groups.append((jnp.concatenate(pairs, axis=1), y_off))
        ds_groups.append(lax.dot_general(xd_state[:, g * gw:(g + 1) * gw], bg, TN_DIMS,
                                         preferred_element_type=F32))
    y_diag = jnp.concatenate([p[0] for p in y_groups], axis=1)
    y_off = jnp.concatenate([p[1] for p in y_groups], axis=1)
    y = y_diag + y_off * expand(jnp.exp(a_cs)) + expand(dskip_ref[...]) * xs
    tot_col = jnp.exp(a_cs_t[:, lc - 1:lc])
    decay_full = jnp.dot(spread_t, jnp.broadcast_to(tot_col, (SSM_HEADS, SSM_STATE)),
                         preferred_element_type=F32, precision=HIGHEST)
    st_ref[...] = decay_full * st_ref[...] + jnp.concatenate(ds_groups, axis=0)
    z = z_ref[...]
    v = y * _silu(z)
    outs = []
    for g in range(SSM_GROUPS):
        vg = v[:, g * gw:(g + 1) * gw]
        outs.append(vg * lax.rsqrt(jnp.mean(vg * vg, axis=-1, keepdims=True) + RMS_EPS))
    y_ref[...] = jnp.concatenate(outs, axis=1) * gssm_ref[...]


def _ssd(xconv, dtx, proj, init_state, layer, a_log, d_skip, g_ssm, seq_len):
    rows = xconv.shape[0]
    n_seq = rows // seq_len
    lc = min(SSM_CHUNK, seq_len)
    nc = seq_len // lc
    zblk = COL_Z // SSM_D_INNER
    vec = lambda a: pl.BlockSpec(a.shape, lambda b, c: (0, 0))
    return pl.pallas_call(
        functools.partial(_ssd_kernel, lc=lc),
        grid=(n_seq, nc),
        in_specs=[pl.BlockSpec((lc, CONV_DIM), lambda b, c: (b * nc + c, 0)),
                  pl.BlockSpec((lc, 128), lambda b, c: (b * nc + c, 0)),
                  pl.BlockSpec((lc, SSM_D_INNER), lambda b, c: (b * nc + c, zblk)),
                  pl.BlockSpec((None, None, SSM_D_INNER, SSM_STATE), lambda b, c: (layer, b, 0, 0)),
                  vec(a_log), vec(d_skip), vec(g_ssm)],
        out_specs=[pl.BlockSpec((lc, SSM_D_INNER), lambda b, c: (b * nc + c, 0)),
                   pl.BlockSpec((None, SSM_D_INNER, SSM_STATE), lambda b, c: (b, 0, 0))],
        out_shape=[jax.ShapeDtypeStruct((rows, SSM_D_INNER), F32),
                   jax.ShapeDtypeStruct((n_seq, SSM_D_INNER, SSM_STATE), F32)],
        compiler_params=_params("parallel", "arbitrary"),
    )(xconv, dtx, proj, init_state, a_log, d_skip, g_ssm)


def _softmax_init(m_scr, l_scr, acc_scr):
    m_scr[...] = jnp.full_like(m_scr, MASK_VALUE)
    l_scr[...] = jnp.zeros_like(l_scr)
    acc_scr[...] = jnp.zeros_like(acc_scr)


def _softmax_step(s, m_scr, l_scr, acc_scr, pv):
    m_prev = m_scr[...]
    m_new = jnp.maximum(m_prev, jnp.max(s, axis=1, keepdims=True))
    alpha = jnp.exp(m_prev - m_new)
    p = jnp.exp(s - m_new)
    l_scr[...] = alpha * l_scr[...] + jnp.sum(p, axis=1, keepdims=True)
    acc_scr[...] = alpha * acc_scr[...] + pv(p.astype(BF16))
    m_scr[...] = m_new


def _transpose_mxu(x):
    c = x.shape[1]
    eye = (lax.broadcasted_iota(jnp.int32, (c, c), 0) == lax.broadcasted_iota(jnp.int32, (c, c), 1))
    return _dot_nt(jnp.where(eye, 1.0, 0.0).astype(BF16), x).astype(BF16)


def _pad_rows(x, n):
    return jnp.concatenate([x, jnp.zeros((n - x.shape[0], x.shape[1]), x.dtype)], axis=0)


def _mla_prompt_kernel(ql_ref, qr_ref, ckv_ref, kr_ref, wuv_ref, o_ref,
                       ckv_b, kr_b, m_scr, l_scr, acc_scr, *, tq, tk):
    i = pl.program_id(1)

    @pl.when(i == 0)
    def _():
        ckv_b[...] = ckv_ref[...].astype(BF16)
        kr_b[...] = kr_ref[...].astype(BF16)

    rows = MLA_HEADS * tq
    q1 = ql_ref[...].reshape(rows, MLA_KV_RANK).astype(BF16)
    q2 = qr_ref[...].reshape(rows, MLA_ROPE_DIM).astype(BF16)
    _softmax_init(m_scr, l_scr, acc_scr)
    q_pos = i * tq + lax.broadcasted_iota(jnp.int32, (rows, tk), 0) % tq
    key_off = lax.broadcasted_iota(jnp.int32, (rows, tk), 1)

    def body(j, carry):
        ks = pl.multiple_of(j * tk, tk)
        kc = ckv_b[pl.ds(ks, tk), :]
        s = (_dot_nt(q1, kc) + _dot_nt(q2, kr_b[pl.ds(ks, tk), :])) * MLA_SCALE
        s = jnp.where(key_off + ks <= q_pos, s, MASK_VALUE)
        _softmax_step(s, m_scr, l_scr, acc_scr, lambda p: _dot(p, kc))
        return carry

    lax.fori_loop(0, (i * tq) // tk + 1, body, 0)
    lat = (acc_scr[...] / l_scr[...]).astype(BF16)
    for h in range(MLA_HEADS):
        o_ref[:, h * MLA_V_DIM:(h + 1) * MLA_V_DIM] = _dot(lat[h * tq:(h + 1) * tq], wuv_ref[h])


def _mla_prompt(q_lat, q_rope, ckv, krope, w_uv_b, n_seq, seq_len):
    tq = 128
    tk = min(512, seq_len)
    nt = seq_len // tq
    rows = n_seq * seq_len
    return pl.pallas_call(
        functools.partial(_mla_prompt_kernel, tq=tq, tk=tk),
        grid=(n_seq, nt),
        in_specs=[pl.BlockSpec((MLA_HEADS, tq, MLA_KV_RANK), lambda b, i: (0, b * nt + i, 0)),
                  pl.BlockSpec((MLA_HEADS, tq, MLA_ROPE_DIM), lambda b, i: (0, b * nt + i, 0)),
                  pl.BlockSpec((seq_len, MLA_KV_RANK), lambda b, i: (b, 0)),
                  pl.BlockSpec((seq_len, MLA_ROPE_DIM), lambda b, i: (b, 0)),
                  pl.BlockSpec(w_uv_b.shape, lambda b, i: (0, 0, 0))],
        out_specs=pl.BlockSpec((tq, MLA_HEADS * MLA_V_DIM), lambda b, i: (b * nt + i, 0)),
        out_shape=jax.ShapeDtypeStruct((rows, MLA_HEADS * MLA_V_DIM), F32),
        scratch_shapes=[pltpu.VMEM((seq_len, MLA_KV_RANK), BF16), pltpu.VMEM((seq_len, MLA_ROPE_DIM), BF16),
                        pltpu.VMEM((MLA_HEADS * tq, 1), F32), pltpu.VMEM((MLA_HEADS * tq, 1), F32),
                        pltpu.VMEM((MLA_HEADS * tq, MLA_KV_RANK), F32)],
        compiler_params=_params("parallel", "arbitrary"),
    )(q_lat, q_rope, ckv, krope, w_uv_b)


def _moba_select(gate, n_valid, block_axis):
    nb = gate.shape[block_axis]
    blk = lax.broadcasted_iota(jnp.int32, gate.shape, block_axis)
    valid = blk < n_valid
    g = jnp.where(valid, gate, -jnp.inf)
    rank = jnp.zeros(gate.shape, jnp.int32)
    for m in range(nb):
        gm = lax.slice_in_dim(g, m, m + 1, axis=block_axis)
        ahead = (gm > g) | ((gm == g) & (m < blk))
        rank = rank + jnp.where(ahead, 1, 0)
    return jnp.where(valid & (rank < MOBA_TOPK), 1.0, 0.0).astype(F32)


def _moba_prompt_kernel(q_ref, k_ref, v_ref, o_ref, k_b, v_b, kmean_scr, m_scr, l_scr, acc_scr, *, nb):
    i = pl.program_id(1)
    tq = MOBA_BLOCK
    rows = MOBA_HEADS * tq

    @pl.when(i == 0)
    def _():
        k = k_ref[...]
        k_b[...] = k.astype(BF16)
        v_b[...] = v_ref[...].astype(BF16)
        kmean_scr[...] = jnp.zeros_like(kmean_scr)
        kmean_scr[0:nb, :] = jnp.mean(k.reshape(nb, MOBA_BLOCK, MOBA_HEAD_DIM), axis=1)

    q = q_ref[...].reshape(rows, MOBA_HEAD_DIM)
    qb = q.astype(BF16)
    gate_t = _dot_nt(kmean_scr[...], q, precision=HIGHEST)
    sel_t = _moba_select(gate_t[0:nb], i, 0)
    sel = jnp.concatenate([sel_t, jnp.zeros((128 - nb, rows), F32)], axis=0).T
    blk_lane = lax.broadcasted_iota(jnp.int32, (rows, 128), 1)
    _softmax_init(m_scr, l_scr, acc_scr)

    def attend(j, mask_fn):
        ks = pl.multiple_of(j * MOBA_BLOCK, MOBA_BLOCK)
        s = _dot_nt(qb, k_b[pl.ds(ks, MOBA_BLOCK), :]) * MOBA_SCALE
        s = jnp.where(mask_fn(), s, MASK_VALUE)
        _softmax_step(s, m_scr, l_scr, acc_scr, lambda p: _dot(p, v_b[pl.ds(ks, MOBA_BLOCK), :]))

    t_q = lax.broadcasted_iota(jnp.int32, (rows, MOBA_BLOCK), 0) % tq
    t_k = lax.broadcasted_iota(jnp.int32, (rows, MOBA_BLOCK), 1)
    attend(i, lambda: t_k <= t_q)

    def body(j, carry):
        picked = jnp.sum(jnp.where(blk_lane == j, sel, 0.0), axis=1, keepdims=True) > 0.5
        attend(j, lambda: jnp.broadcast_to(picked, (rows, MOBA_BLOCK)))
        return carry

    lax.fori_loop(0, i, body, 0)
    out = acc_scr[...] / l_scr[...]
    for h in range(MOBA_HEADS):
        o_ref[:, h * MOBA_HEAD_DIM:(h + 1) * MOBA_HEAD_DIM] = out[h * tq:(h + 1) * tq]


def _moba_prompt(qm, km, proj, n_seq, seq_len):
    nb = seq_len // MOBA_BLOCK
    rows = n_seq * seq_len
    vblk = COL_VM // MOBA_HEAD_DIM
    return pl.pallas_call(
        functools.partial(_moba_prompt_kernel, nb=nb),
        grid=(n_seq, nb),
        in_specs=[pl.BlockSpec((MOBA_HEADS, MOBA_BLOCK, MOBA_HEAD_DIM), lambda b, i: (0, b * nb + i, 0)),
                  pl.BlockSpec((seq_len, MOBA_HEAD_DIM), lambda b, i: (b, 0)),
                  pl.BlockSpec((seq_len, MOBA_HEAD_DIM), lambda b, i: (b, vblk))],
        out_specs=pl.BlockSpec((MOBA_BLOCK, MOBA_HEADS * MOBA_HEAD_DIM), lambda b, i: (b * nb + i, 0)),
        out_shape=jax.ShapeDtypeStruct((rows, MOBA_HEADS * MOBA_HEAD_DIM), F32),
        scratch_shapes=[pltpu.VMEM((seq_len, MOBA_HEAD_DIM), BF16), pltpu.VMEM((seq_len, MOBA_HEAD_DIM), BF16),
                        pltpu.VMEM((128, MOBA_HEAD_DIM), F32),
                        pltpu.VMEM((MOBA_HEADS * MOBA_BLOCK, 1), F32), pltpu.VMEM((MOBA_HEADS * MOBA_BLOCK, 1), F32),
                        pltpu.VMEM((MOBA_HEADS * MOBA_BLOCK, MOBA_HEAD_DIM), F32)],
        compiler_params=_params("parallel", "arbitrary"),
    )(qm, km, proj)


SAMPLE_CHUNKS = 4
SCORE_SPLIT = 2


def _attn_sample_kernel(pt_ref, ckv_pool, kr_pool, mk_pool, mv_pool,
                        ql_ref, qr_ref, qm_ref, cnew_ref, knew_ref, kmnew_ref, vmnew_ref, wuv_ref,
                        mla_o_ref, moba_o_ref,
                        ckv_buf, kr_buf, mk_buf, mv_buf, kstash, kmean_scr, selmask,
                        m1, l1, acc1, m2, l2, acc2, sems, *, layer, t_new, ch):
    seq = pl.program_id(0)
    n_seq = pl.num_programs(0)
    rows = MLA_HEADS * t_new
    ckeys = ch * PAGE_SIZE
    nb_chunk = ckeys // MOBA_BLOCK

    def copies_a(page_of, slot):
        out = []
        for j in range(ch):
            page, win = page_of(j), pl.ds(j * PAGE_SIZE, PAGE_SIZE)
            out.append(pltpu.make_async_copy(ckv_pool.at[layer, page], ckv_buf.at[slot, win, :], sems.at[0, slot]))
            out.append(pltpu.make_async_copy(kr_pool.at[layer, page], kr_buf.at[slot, :, win], sems.at[1, slot]))
            out.append(pltpu.make_async_copy(mk_pool.at[layer, page], mk_buf.at[slot, win, :], sems.at[2, slot]))
        return out

    def copies_b(page_of, slot):
        return [pltpu.make_async_copy(mv_pool.at[layer, page_of(j)],
                                      mv_buf.at[slot, pl.ds(j * PAGE_SIZE, PAGE_SIZE), :], sems.at[3, slot])
                for j in range(ch)]

    def start(copies, sq, k):
        for c in copies(lambda j: pt_ref[sq, k * ch + j], k % 2):
            c.start()

    def wait(copies, k):
        for c in copies(lambda j: 0, k % 2):
            c.wait()

    def refill(copies, k):
        if k + 2 < SAMPLE_CHUNKS:
            start(copies, seq, k + 2)
        else:
            @pl.when(seq + 1 < n_seq)
            def _():
                start(copies, seq + 1, k + 2 - SAMPLE_CHUNKS)

    @pl.when(seq == 0)
    def _():
        for k in range(2):
            start(copies_a, seq, k)
            start(copies_b, seq, k)

    t_q = lax.broadcasted_iota(jnp.int32, (rows, t_new), 0) % t_new
    t_k = lax.broadcasted_iota(jnp.int32, (rows, t_new), 1)
    causal_new = t_k <= t_q

    def scores(q_rows, keys):
        n = keys.shape[0] // SCORE_SPLIT
        return jnp.concatenate([_dot_nt(q_rows, keys[u * n:(u + 1) * n]) for u in range(SCORE_SPLIT)], axis=1)

    q1 = ql_ref[...].reshape(rows, MLA_KV_RANK).astype(BF16)
    q2 = qr_ref[...].reshape(rows, MLA_ROPE_DIM).astype(BF16)
    _softmax_init(m1, l1, acc1)
    for k in range(SAMPLE_CHUNKS):
        slot = k % 2
        wait(copies_a, k)
        kc = ckv_buf[slot].astype(BF16)
        s = (scores(q1, kc) + _dot(q2, kr_buf[slot].astype(BF16))) * MLA_SCALE
        _softmax_step(s, m1, l1, acc1, lambda p: _dot(p, kc))
        mk = mk_buf[slot]
        kstash[k * ckeys:(k + 1) * ckeys, :] = mk.astype(BF16)
        kmean_scr[k * nb_chunk:(k + 1) * nb_chunk, :] = jnp.mean(
            mk.reshape(nb_chunk, MOBA_BLOCK, MOBA_HEAD_DIM), axis=1)
        refill(copies_a, k)

    cn = cnew_ref[...].astype(BF16)
    sn = (_dot_nt(q1, cn) + _dot_nt(q2, knew_ref[...].astype(BF16))) * MLA_SCALE
    _softmax_step(jnp.where(causal_new, sn, MASK_VALUE), m1, l1, acc1, lambda p: _dot(p, cn))
    lat = (acc1[...] / l1[...]).astype(BF16)
    for h in range(MLA_HEADS):
        mla_o_ref[:, h * MLA_V_DIM:(h + 1) * MLA_V_DIM] = _dot(lat[h * t_new:(h + 1) * t_new], wuv_ref[h])

    q = qm_ref[...].reshape(rows, MOBA_HEAD_DIM)
    qb = q.astype(BF16)
    _softmax_init(m2, l2, acc2)
    vn = vmnew_ref[...].astype(BF16)
    sm = _dot_nt(qb, kmnew_ref[...].astype(BF16)) * MOBA_SCALE
    _softmax_step(jnp.where(causal_new, sm, MASK_VALUE), m2, l2, acc2, lambda p: _dot(p, vn))
    nb_past = SAMPLE_CHUNKS * nb_chunk
    gate = _dot_nt(q, kmean_scr[...], precision=HIGHEST)
    sel = _moba_select(gate, nb_past, 1)
    for b in range(nb_past):
        selmask[:, b * MOBA_BLOCK:(b + 1) * MOBA_BLOCK] = jnp.broadcast_to(sel[:, b:b + 1], (rows, MOBA_BLOCK))

    for k in range(SAMPLE_CHUNKS):
        slot = k % 2
        wait(copies_b, k)
        v = mv_buf[slot].astype(BF16)
        s = scores(qb, kstash[k * ckeys:(k + 1) * ckeys, :]) * MOBA_SCALE
        s = jnp.where(selmask[:, k * ckeys:(k + 1) * ckeys] > 0.5, s, MASK_VALUE)
        _softmax_step(s, m2, l2, acc2, lambda p: _dot(p, v))
        refill(copies_b, k)

    out = acc2[...] / l2[...]
    for h in range(MOBA_HEADS):
        moba_o_ref[:, h * MOBA_HEAD_DIM:(h + 1) * MOBA_HEAD_DIM] = out[h * t_new:(h + 1) * t_new]


def _attn_sample(page_table, ckv_pool, kr_pool_t, mk_pool, mv_pool, layer,
                 q_lat, q_rope, qm, ckv_new, kr_new, km_new, proj, w_uv_b, t_new):
    n_seq, n_pages = page_table.shape
    assert n_pages % (SAMPLE_CHUNKS * MOBA_BLOCK // PAGE_SIZE) == 0
    ch = n_pages // SAMPLE_CHUNKS
    ckeys = ch * PAGE_SIZE
    past_len = n_pages * PAGE_SIZE
    rows = MLA_HEADS * t_new
    vblk = COL_VM // MOBA_HEAD_DIM
    hbm = pl.BlockSpec(memory_space=pl.ANY)
    heads = lambda w: pl.BlockSpec((MLA_HEADS, t_new, w), lambda s, pt: (0, s, 0))
    new_rows = lambda w, blk=0: pl.BlockSpec((t_new, w), lambda s, pt: (s, blk))
    grid_spec = pltpu.PrefetchScalarGridSpec(
        num_scalar_prefetch=1, grid=(n_seq,),
        in_specs=[hbm, hbm, hbm, hbm,
                  heads(MLA_KV_RANK), heads(MLA_ROPE_DIM), heads(MOBA_HEAD_DIM),
                  new_rows(MLA_KV_RANK), new_rows(MLA_ROPE_DIM), new_rows(MOBA_HEAD_DIM),
                  new_rows(MOBA_HEAD_DIM, vblk),
                  pl.BlockSpec(w_uv_b.shape, lambda s, pt: (0, 0, 0))],
        out_specs=[new_rows(MLA_HEADS * MLA_V_DIM), new_rows(MOBA_HEADS * MOBA_HEAD_DIM)],
        scratch_shapes=[pltpu.VMEM((2, ckeys, MLA_KV_RANK), F32),
                        pltpu.VMEM((2, MLA_ROPE_DIM, ckeys), F32),
                        pltpu.VMEM((2, ckeys, MOBA_HEAD_DIM), F32),
                        pltpu.VMEM((2, ckeys, MOBA_HEAD_DIM), F32),
                        pltpu.VMEM((past_len, MOBA_HEAD_DIM), BF16),
                        pltpu.VMEM((past_len // MOBA_BLOCK, MOBA_HEAD_DIM), F32),
                        pltpu.VMEM((rows, past_len), F32),
                        pltpu.VMEM((rows, 1), F32), pltpu.VMEM((rows, 1), F32),
                        pltpu.VMEM((rows, MLA_KV_RANK), F32),
                        pltpu.VMEM((rows, 1), F32), pltpu.VMEM((rows, 1), F32),
                        pltpu.VMEM((rows, MOBA_HEAD_DIM), F32),
                        pltpu.SemaphoreType.DMA((4, 2))])
    return pl.pallas_call(
        functools.partial(_attn_sample_kernel, layer=layer, t_new=t_new, ch=ch),
        grid_spec=grid_spec,
        out_shape=[jax.ShapeDtypeStruct((n_seq * t_new, MLA_HEADS * MLA_V_DIM), F32),
                   jax.ShapeDtypeStruct((n_seq * t_new, MOBA_HEADS * MOBA_HEAD_DIM), F32)],
        compiler_params=_params("arbitrary"),
    )(page_table, ckv_pool, kr_pool_t, mk_pool, mv_pool, q_lat, q_rope, qm, ckv_new, kr_new, km_new, proj, w_uv_b)


def _outproj_kernel(mla_ref, moba_ref, ssm_ref, w_ref, x_ref, gt_ref, lg_ref, lb_ref, o_ref, *, alpha):
    w_mla = MLA_HEADS * MLA_V_DIM
    w_moba = MOBA_HEADS * MOBA_HEAD_DIM
    mixed = (_dot(mla_ref[...].astype(BF16), w_ref[0:w_mla, :])
             + _dot(moba_ref[...].astype(BF16), w_ref[w_mla:w_mla + w_moba, :])
             + _dot(ssm_ref[...].astype(BF16), w_ref[w_mla + w_moba:, :]))
    v = alpha * x_ref[...] + _mod_rows(gt_ref, x_ref.shape[0]) * mixed
    o_ref[...] = _layer_norm_rows(v, lg_ref[...], lb_ref[...])


def _outproj(mla_out, moba_out, ssm_out, w_out_b, layer, x, mod, ln_g, ln_b, alpha, tm):
    rows = x.shape[0]
    row = lambda w: pl.BlockSpec((tm, w), lambda i: (i, 0))
    vec_spec = pl.BlockSpec((1, D_MODEL), lambda i: (0, 0))
    return pl.pallas_call(
        functools.partial(_outproj_kernel, alpha=alpha),
        grid=(rows // tm,),
        in_specs=[row(mla_out.shape[1]), row(moba_out.shape[1]), row(ssm_out.shape[1]),
                  pl.BlockSpec((None, D_MODEL, D_MODEL), lambda i: (layer, 0, 0)),
                  row(D_MODEL), mod.spec(tm), vec_spec, vec_spec],
        out_specs=row(D_MODEL),
        out_shape=jax.ShapeDtypeStruct((rows, D_MODEL), F32),
        compiler_params=_params("parallel"),
    )(mla_out, moba_out, ssm_out, w_out_b, x, mod.arrays[2], ln_g, ln_b)


def _trunk(x, mods_by_layer, seq_len, tabs, conv_init, ssm_init, attn, w, tm):
    depth = w["w_in_p"].shape[0]
    alpha = (2 * depth) ** 0.25
    n_seq = x.shape[0] // seq_len
    tab_tiles = tabs[0].shape[0] // tm
    states = []
    for l in range(depth):
        mods = mods_by_layer[l]
        ln_g, ln_b = w["ln_g"][l], w["ln_b"][l]
        x = _ffn(x, mods[0], w["wg"], w["wu"], w["wd"], ln_g[0:1], ln_b[0:1], l, 0, alpha, tm)
        proj = _inproj(x, mods[1], w["w_in_p"], l, tm)
        q_lat, q_rope, ckv, krope, qm, km, dtx = _prep(
            proj, tabs, tab_tiles, w["g_q"][l:l + 1], w["w_uq_p"][l], w["g_kv"][l:l + 1], w["w_uk_t"][l],
            w["dt_bias_p"][l:l + 1], tm)
        conv_tm = min(tm, seq_len)
        init8 = jnp.pad(conv_init[l], ((0, 0), (8 - (CONV_WIDTH - 1), 0), (0, 0)))
        xconv = _conv(proj, init8, w["conv_w"][l], w["conv_b"][l:l + 1], seq_len, conv_tm)
        ssm_out, ssm_state = _ssd(xconv, dtx, proj, ssm_init, l, w["a_log"][l:l + 1], w["d_skip"][l:l + 1],
                                  w["g_ssm"][l:l + 1], seq_len)
        mla_out, moba_out = attn(l, q_lat, q_rope, ckv, krope, qm, km, proj)
        x = _outproj(mla_out, moba_out, ssm_out, w["w_out_b"], l, x, mods[1], ln_g[1:2], ln_b[1:2], alpha, tm)
        x = _ffn(x, mods[2], w["wg"], w["wu"], w["wd"], ln_g[2:3], ln_b[2:3], l, 1, alpha, tm)
        vm = proj[:, COL_VM:COL_VM + MOBA_HEAD_DIM]
        keep = CONV_WIDTH - 1
        xbc_tail = proj.reshape(n_seq, seq_len, IN_COLS_PADDED)[:, seq_len - min(keep, seq_len):,
                                                                COL_XBC:COL_XBC + CONV_DIM]
        conv_state = jnp.concatenate([conv_init[l], xbc_tail], axis=1)[:, -keep:]
        states.append((ckv.reshape(n_seq, seq_len, MLA_KV_RANK),
                       krope.reshape(n_seq, seq_len, MLA_ROPE_DIM),
                       km.reshape(n_seq, seq_len, 1, MOBA_HEAD_DIM),
                       vm.reshape(n_seq, seq_len, 1, MOBA_HEAD_DIM),
                       ssm_state.reshape(n_seq, SSM_HEADS, SSM_HEAD_DIM, SSM_STATE),
                       conv_state))
    return x, [jnp.stack([s[i] for s in states]) for i in range(len(states[0]))]


def _row_tile(rows, seq_len, largest):
    for tm in (512, 256, 128, 64, 32, 16, 8):
        if tm <= largest and rows % tm == 0 and (seq_len % tm == 0 or tm % seq_len == 0):
            return tm
    raise ValueError("row count must be a multiple of 8")


def kernel(x_prompt, x_sample, cache_mla_ckv, cache_mla_krope, cache_moba_k, cache_moba_v, state_ssm, state_conv,
           page_table, c_prompt, c_sample, w_ada, b_ada, ln_g, ln_b, ffn_w_gate, ffn_w_up, ffn_w_down,
           w_in, g_q, w_uq, g_kv, w_uk, w_uv, conv_w, conv_b, dt_bias, a_log, d_skip, g_ssm, w_out):
    depth = w_in.shape[0]
    n_prompt, seq_len, _ = x_prompt.shape
    n_sample, t_new, _ = x_sample.shape
    n_pages = page_table.shape[1]
    past_len = n_pages * PAGE_SIZE
    n_pool = cache_mla_ckv.shape[1]

    assert past_len % MOBA_BLOCK == 0 and t_new <= MOBA_BLOCK and seq_len % MOBA_BLOCK == 0
    names = ["cq", "ckv", "kr", "qm", "km", "vm", "z", "xbc", "dt"]
    widths = [MLA_Q_RANK, MLA_KV_RANK, MLA_ROPE_DIM, MOBA_HEADS * MOBA_HEAD_DIM, MOBA_HEAD_DIM, MOBA_HEAD_DIM,
              SSM_D_INNER, CONV_DIM, SSM_HEADS]
    pieces, start = {}, 0
    for name, width in zip(names, widths):
        pieces[name] = (start, start + width)
        start += width
    order = ["cq", "qm", "ckv", "km", "vm", "xbc", "z", "kr", "dt"]
    w_in_p = jnp.concatenate([w_in[:, :, pieces[k][0]:pieces[k][1]] for k in order]
                             + [jnp.zeros((depth, D_MODEL, IN_COLS_PADDED - w_in.shape[2]), w_in.dtype)],
                             axis=2).astype(BF16)
    qk = MLA_NOPE_DIM + MLA_ROPE_DIM
    w_uq_h = w_uq.reshape(depth, MLA_Q_RANK, MLA_HEADS, qk)
    w_uq_p = jnp.concatenate([w_uq_h[..., :MLA_NOPE_DIM].reshape(depth, MLA_Q_RANK, MLA_HEADS * MLA_NOPE_DIM),
                              w_uq_h[..., MLA_NOPE_DIM:].reshape(depth, MLA_Q_RANK, MLA_HEADS * MLA_ROPE_DIM)],
                             axis=2).astype(BF16)
    dt_bias_p = jnp.zeros((depth, 128), F32).at[:, DT_LANE:DT_LANE + SSM_HEADS].set(dt_bias)
    w = dict(
        wg=ffn_w_gate.astype(BF16), wu=ffn_w_up.astype(BF16), wd=ffn_w_down.astype(BF16),
        w_in_p=w_in_p, w_uq_p=w_uq_p, w_uk_t=jnp.swapaxes(w_uk, 2, 3).astype(BF16), w_uv_b=w_uv.astype(BF16),
        w_out_b=w_out.astype(BF16), g_q=g_q, g_kv=g_kv, ln_g=ln_g, ln_b=ln_b,
        conv_w=conv_w.reshape(depth, CONV_WIDTH, CONV_DIM), conv_b=conv_b, dt_bias_p=dt_bias_p,
        a_log=a_log, d_skip=d_skip, g_ssm=g_ssm)

    n_c = n_prompt + n_sample
    c_rows = -(-n_c // 8) * 8
    c_all = jnp.concatenate([c_prompt, c_sample, jnp.zeros((c_rows - n_c, D_MODEL), F32)], axis=0)
    mod = _ada(c_all, w_ada, b_ada)
    mods_p = [_make_mods(mod[l, :n_prompt], seq_len) for l in range(depth)]
    mods_s = [_make_mods(mod[l, n_prompt:n_c], t_new) for l in range(depth)]

    tabs_p = _rope_tables(jnp.arange(seq_len, dtype=jnp.int32))
    conv0 = jnp.zeros((depth, n_prompt, CONV_WIDTH - 1, CONV_DIM), F32)
    ssm0 = jnp.zeros((depth, n_prompt, SSM_D_INNER, SSM_STATE), F32)

    def attn_prompt(l, q_lat, q_rope, ckv, krope, qm, km, proj):
        return (_mla_prompt(q_lat, q_rope, ckv, krope, w["w_uv_b"][l], n_prompt, seq_len),
                _moba_prompt(qm, km, proj, n_prompt, seq_len))

    tm_p = _row_tile(n_prompt * seq_len, seq_len, 512)
    y_p, st_p = _trunk(x_prompt.reshape(n_prompt * seq_len, D_MODEL), mods_p, seq_len, tabs_p, conv0, ssm0,
                       attn_prompt, w, tm_p)

    pos_s = past_len + jnp.arange(t_new, dtype=jnp.int32)
    tabs_s = tuple(jnp.tile(t, (n_sample, 1)) for t in _rope_tables(pos_s))
    mk_pool = cache_moba_k.reshape(depth, n_pool, PAGE_SIZE, MOBA_HEAD_DIM)
    mv_pool = cache_moba_v.reshape(depth, n_pool, PAGE_SIZE, MOBA_HEAD_DIM)

    kr_pool_t = jnp.swapaxes(cache_mla_krope, 2, 3)

    def attn_sample(l, q_lat, q_rope, ckv, krope, qm, km, proj):
        return _attn_sample(page_table, cache_mla_ckv, kr_pool_t, mk_pool, mv_pool, l,
                            q_lat, q_rope, qm, ckv, krope, km, proj, w["w_uv_b"][l], t_new)

    tm_s = _row_tile(n_sample * t_new, t_new, 512)
    ssm_init_s = state_ssm.reshape(depth, n_sample, SSM_D_INNER, SSM_STATE)
    y_s, st_s = _trunk(x_sample.reshape(n_sample * t_new, D_MODEL), mods_s, t_new, tabs_s, state_conv, ssm_init_s,
                       attn_sample, w, tm_s)

    return (y_p.reshape(n_prompt, seq_len, D_MODEL), y_s.reshape(n_sample, t_new, D_MODEL),
            *st_p, *st_s)
```

```python
import functools
import math

import jax
import jax.numpy as jnp
from jax import lax
from jax.experimental import pallas as pl
from jax.experimental.pallas import tpu as pltpu

F32 = jnp.float32
BF16 = jnp.bfloat16

D_MODEL = 2048
PAGE_SIZE = 128
MLA_HEADS = 4
MLA_Q_RANK = 512
MLA_KV_RANK = 256
MLA_NOPE_DIM = 128
MLA_ROPE_DIM = 64
MLA_V_DIM = 128
MLA_SCALE = (MLA_NOPE_DIM + MLA_ROPE_DIM) ** -0.5
MOBA_HEADS = 4
MOBA_HEAD_DIM = 128
MOBA_BLOCK = 256
MOBA_TOPK = 3
MOBA_SCALE = MOBA_HEAD_DIM ** -0.5
SSM_HEADS = 16
SSM_HEAD_DIM = 64
SSM_D_INNER = SSM_HEADS * SSM_HEAD_DIM
SSM_GROUPS = 2
SSM_STATE = 128
SSM_CHUNK = 128
CONV_WIDTH = 4
CONV_DIM = SSM_D_INNER + 2 * SSM_GROUPS * SSM_STATE
MACARON_WEIGHT = 0.5
ROPE_THETA = 10000.0
LN_EPS = 1e-5
RMS_EPS = 1e-6

COL_CQ = 0
COL_QM = 512
COL_CKV = 1024
COL_KM = 1280
COL_VM = 1408
COL_XBC = 1536
COL_Z = 3072
COL_KRDT = 4096
IN_COLS_PADDED = 4224
DT_LANE = 64

MASK_VALUE = -1e30
VMEM_LIMIT = 56 * 1024 * 1024

HIGHEST = lax.Precision.HIGHEST
NT_DIMS = (((1,), (1,)), ((), ()))
TN_DIMS = (((0,), (0,)), ((), ()))


def _params(*sem):
    return pltpu.CompilerParams(dimension_semantics=sem, vmem_limit_bytes=VMEM_LIMIT)


def _silu(v):
    return v * jax.nn.sigmoid(v)


def _dot(a, b):
    return jnp.dot(a, b, preferred_element_type=F32)


def _dot_nt(a, b, precision=None):
    return lax.dot_general(a, b, NT_DIMS, preferred_element_type=F32, precision=precision)


def _layer_norm_rows(v, g, b):
    mu = jnp.mean(v, axis=-1, keepdims=True)
    d = v - mu
    var = jnp.mean(d * d, axis=-1, keepdims=True)
    return d * lax.rsqrt(var + LN_EPS) * g + b


def _ada_kernel(c_ref, w_ref, b_ref, o_ref):
    h = _silu(c_ref[...]).astype(BF16)
    o_ref[...] = _dot(h, w_ref[...].astype(BF16)) + b_ref[...]


def _ada(c_all, w_ada, b_ada):
    depth, d, n = w_ada.shape
    rows = c_all.shape[0]
    tn = 1024
    return pl.pallas_call(
        _ada_kernel,
        grid=(depth, n // tn),
        in_specs=[
            pl.BlockSpec((rows, d), lambda l, j: (0, 0)),
            pl.BlockSpec((None, d, tn), lambda l, j: (l, 0, j)),
            pl.BlockSpec((None, 1, tn), lambda l, j: (l, 0, j)),
        ],
        out_specs=pl.BlockSpec((None, rows, tn), lambda l, j: (l, 0, j)),
        out_shape=jax.ShapeDtypeStruct((depth, rows, n), F32),
        compiler_params=_params("parallel", "parallel"),
    )(c_all, w_ada, b_ada.reshape(depth, 1, n))


class _Mod:
    def __init__(self, arrays, seq_len):
        self.arrays = arrays
        self.seq_len = seq_len

    def spec(self, tm):
        if self.seq_len >= tm:
            tps = self.seq_len // tm
            return pl.BlockSpec((None, 1, D_MODEL), lambda i, *_: (i // tps, 0, 0))
        return pl.BlockSpec((tm // self.seq_len, 1, D_MODEL), lambda i, *_: (i, 0, 0))


def _mod_rows(m_ref, tm):
    m = m_ref[...]
    if m.ndim == 2:
        return m
    g = m.shape[0]
    return jnp.broadcast_to(m, (g, tm // g, D_MODEL)).reshape(tm, D_MODEL)


def _make_mods(mod, seq_len):
    n_seq = mod.shape[0]
    m = mod.reshape(n_seq, 3, 3, 1, D_MODEL)
    return [_Mod([m[:, j, k] for k in range(3)], seq_len) for j in range(3)]


def _ffn_kernel(x_ref, sh_ref, sc_ref, gt_ref, wg_ref, wu_ref, wd_ref, lg_ref, lb_ref, o_ref,
                h_scr, acc_scr, *, alpha, weight):
    j = pl.program_id(1)
    tm = x_ref.shape[0]

    @pl.when(j == 0)
    def _():
        h_scr[...] = (x_ref[...] * (1.0 + _mod_rows(sc_ref, tm)) + _mod_rows(sh_ref, tm)).astype(BF16)
        acc_scr[...] = jnp.zeros_like(acc_scr)

    h = h_scr[...]
    a = _dot(h, wg_ref[...])
    u = _dot(h, wu_ref[...])
    act = (_silu(a) * u).astype(BF16)
    acc_scr[...] += _dot(act, wd_ref[...])

    @pl.when(j == pl.num_programs(1) - 1)
    def _():
        v = alpha * x_ref[...] + weight * _mod_rows(gt_ref, tm) * acc_scr[...]
        o_ref[...] = _layer_norm_rows(v, lg_ref[...], lb_ref[...])


def _ffn(x, mod, wg, wu, wd, ln_g, ln_b, layer, slot, alpha, tm):
    rows = x.shape[0]
    d_ff = wg.shape[-1]
    tf = 512 if d_ff % 512 == 0 else d_ff
    row_spec = pl.BlockSpec((tm, D_MODEL), lambda i, j: (i, 0))
    vec_spec = pl.BlockSpec((1, D_MODEL), lambda i, j: (0, 0))
    return pl.pallas_call(
        functools.partial(_ffn_kernel, alpha=alpha, weight=MACARON_WEIGHT),
        grid=(rows // tm, d_ff // tf),
        in_specs=[row_spec, mod.spec(tm), mod.spec(tm), mod.spec(tm),
                  pl.BlockSpec((None, None, D_MODEL, tf), lambda i, j: (layer, slot, 0, j)),
                  pl.BlockSpec((None, None, D_MODEL, tf), lambda i, j: (layer, slot, 0, j)),
                  pl.BlockSpec((None, None, tf, D_MODEL), lambda i, j: (layer, slot, j, 0)),
                  vec_spec, vec_spec],
        out_specs=row_spec,
        out_shape=jax.ShapeDtypeStruct((rows, D_MODEL), F32),
        scratch_shapes=[pltpu.VMEM((tm, D_MODEL), BF16), pltpu.VMEM((tm, D_MODEL), F32)],
        compiler_params=_params("parallel", "arbitrary"),
    )(x, mod.arrays[0], mod.arrays[1], mod.arrays[2], wg, wu, wd, ln_g, ln_b)


def _inproj_kernel(x_ref, sh_ref, sc_ref, w_ref, o_ref):
    tm = x_ref.shape[0]
    h = (x_ref[...] * (1.0 + _mod_rows(sc_ref, tm)) + _mod_rows(sh_ref, tm)).astype(BF16)
    o_ref[...] = _dot(h, w_ref[...])


def _inproj(x, mod, w_in_p, layer, tm):
    rows = x.shape[0]
    tn = IN_COLS_PADDED // 3
    return pl.pallas_call(
        _inproj_kernel,
        grid=(rows // tm, 3),
        in_specs=[pl.BlockSpec((tm, D_MODEL), lambda i, j: (i, 0)), mod.spec(tm), mod.spec(tm),
                  pl.BlockSpec((None, D_MODEL, tn), lambda i, j: (layer, 0, j))],
        out_specs=pl.BlockSpec((tm, tn), lambda i, j: (i, j)),
        out_shape=jax.ShapeDtypeStruct((rows, IN_COLS_PADDED), F32),
        compiler_params=_params("parallel", "arbitrary"),
    )(x, mod.arrays[0], mod.arrays[1], w_in_p)


def _rope64(v, cos, sin):
    lane = lax.broadcasted_iota(jnp.int32, v.shape, 1)
    swap = jnp.where(lane % 64 < 32, pltpu.roll(v, 96, axis=1), pltpu.roll(v, 32, axis=1))
    return v * cos + swap * sin


def _rope128(v, cos, sin):
    return v * cos + pltpu.roll(v, 64, axis=1) * sin


def _rms(v, g):
    return v * lax.rsqrt(jnp.mean(v * v, axis=-1, keepdims=True) + RMS_EPS) * g


def _prep_kernel(cq_ref, qm_ref, ckvr_ref, km_ref, krdt_ref, cosa_ref, sina_ref, cosb_ref, sinb_ref,
                 gq_ref, wuq_ref, gkv_ref, wuk_ref, dtb_ref,
                 qlat_ref, qrope_ref, ckv_ref, krope_ref, qmo_ref, kmo_ref, dtx_ref):
    cosa, sina = cosa_ref[...], sina_ref[...]
    cosb, sinb = cosb_ref[...], sinb_ref[...]
    cqn = _rms(cq_ref[...], gq_ref[...]).astype(BF16)
    q = _dot(cqn, wuq_ref[...])
    nope_w = MLA_HEADS * MLA_NOPE_DIM
    for pair in range(MLA_HEADS // 2):
        r = _rope64(q[:, nope_w + pair * 128: nope_w + (pair + 1) * 128], cosa, sina)
        qrope_ref[2 * pair] = r[:, :MLA_ROPE_DIM]
        qrope_ref[2 * pair + 1] = r[:, MLA_ROPE_DIM:]
    for h in range(MLA_HEADS):
        qn = q[:, h * MLA_NOPE_DIM:(h + 1) * MLA_NOPE_DIM].astype(BF16)
        qlat_ref[h] = _dot(qn, wuk_ref[h])
    ckv_ref[...] = _rms(ckvr_ref[...], gkv_ref[...])
    krdt = krdt_ref[...]
    krope_ref[...] = _rope64(krdt, cosa, sina)[:, :MLA_ROPE_DIM]
    dtx_ref[...] = jax.nn.softplus(krdt + dtb_ref[...])
    qm = qm_ref[...]
    for h in range(MOBA_HEADS):
        qmo_ref[h] = _rope128(qm[:, h * MOBA_HEAD_DIM:(h + 1) * MOBA_HEAD_DIM], cosb, sinb)
    kmo_ref[...] = _rope128(km_ref[...], cosb, sinb)


def _prep(proj, tabs, tab_tiles, g_q, w_uq_p, g_kv, w_uk_t, dt_bias_p, tm):
    rows = proj.shape[0]

    def col(width, start):
        return pl.BlockSpec((tm, width), lambda i: (i, start // width))

    tab_spec = pl.BlockSpec((tm, 128), lambda i: (i % tab_tiles, 0))

    def full(a):
        return pl.BlockSpec(a.shape, lambda i: (0,) * a.ndim)

    heads_spec = lambda w: pl.BlockSpec((MLA_HEADS, tm, w), lambda i: (0, i, 0))
    row_spec = lambda w: pl.BlockSpec((tm, w), lambda i: (i, 0))
    return pl.pallas_call(
        _prep_kernel,
        grid=(rows // tm,),
        in_specs=[col(512, COL_CQ), col(512, COL_QM), col(256, COL_CKV), col(128, COL_KM), col(128, COL_KRDT),
                  tab_spec, tab_spec, tab_spec, tab_spec,
                  full(g_q), full(w_uq_p), full(g_kv), full(w_uk_t), full(dt_bias_p)],
        out_specs=[heads_spec(MLA_KV_RANK), heads_spec(MLA_ROPE_DIM), row_spec(MLA_KV_RANK), row_spec(MLA_ROPE_DIM),
                   heads_spec(MOBA_HEAD_DIM), row_spec(MOBA_HEAD_DIM), row_spec(128)],
        out_shape=[jax.ShapeDtypeStruct((MLA_HEADS, rows, MLA_KV_RANK), F32),
                   jax.ShapeDtypeStruct((MLA_HEADS, rows, MLA_ROPE_DIM), F32),
                   jax.ShapeDtypeStruct((rows, MLA_KV_RANK), F32),
                   jax.ShapeDtypeStruct((rows, MLA_ROPE_DIM), F32),
                   jax.ShapeDtypeStruct((MOBA_HEADS, rows, MOBA_HEAD_DIM), F32),
                   jax.ShapeDtypeStruct((rows, MOBA_HEAD_DIM), F32),
                   jax.ShapeDtypeStruct((rows, 128), F32)],
        compiler_params=_params("parallel"),
    )(proj, proj, proj, proj, proj, *tabs, g_q, w_uq_p, g_kv, w_uk_t, dt_bias_p)


def _rope_tables(pos):
    def tab(half):
        inv = ROPE_THETA ** (-jnp.arange(half, dtype=F32) / half)
        ang = pos.astype(F32)[:, None] * inv[None, :]
        return jnp.cos(ang), jnp.sin(ang)

    c32, s32 = tab(MLA_ROPE_DIM // 2)
    c64, s64 = tab(MOBA_HEAD_DIM // 2)
    cosa = jnp.concatenate([c32, c32, c32, c32], axis=1)
    sina = jnp.concatenate([-s32, s32, -s32, s32], axis=1)
    cosb = jnp.concatenate([c64, c64], axis=1)
    sinb = jnp.concatenate([-s64, s64], axis=1)
    return cosa, sina, cosb, sinb


def _conv_kernel(cur_ref, prev_ref, init_ref, w_ref, b_ref, o_ref, pad_scr, *, tiles_per_seq, tm):
    i = pl.program_id(0)
    at_start = (i % tiles_per_seq) == 0
    pad_scr[0:8, :] = jnp.where(at_start, init_ref[...], prev_ref[...])
    pad_scr[8:8 + tm, :] = cur_ref[...]
    acc = b_ref[...] + w_ref[CONV_WIDTH - 1:CONV_WIDTH, :] * cur_ref[...]
    for k in range(CONV_WIDTH - 1):
        shift = CONV_WIDTH - 1 - k
        acc = acc + w_ref[k:k + 1, :] * pad_scr[8 - shift:8 - shift + tm, :]
    o_ref[...] = _silu(acc)


def _conv(proj, init8, conv_w, conv_b, seq_len, tm):
    rows = proj.shape[0]
    tiles_per_seq = seq_len // tm
    blk = COL_XBC // CONV_DIM
    return pl.pallas_call(
        functools.partial(_conv_kernel, tiles_per_seq=tiles_per_seq, tm=tm),
        grid=(rows // tm,),
        in_specs=[pl.BlockSpec((tm, CONV_DIM), lambda i: (i, blk)),
                  pl.BlockSpec((8, CONV_DIM), lambda i: (jnp.maximum(i * (tm // 8) - 1, 0), blk)),
                  pl.BlockSpec((None, 8, CONV_DIM), lambda i: (i // tiles_per_seq, 0, 0)),
                  pl.BlockSpec((CONV_WIDTH, CONV_DIM), lambda i: (0, 0)),
                  pl.BlockSpec((1, CONV_DIM), lambda i: (0, 0))],
        out_specs=pl.BlockSpec((tm, CONV_DIM), lambda i: (i, 0)),
        out_shape=jax.ShapeDtypeStruct((rows, CONV_DIM), F32),
        scratch_shapes=[pltpu.VMEM((tm + 8, CONV_DIM), F32)],
        compiler_params=_params("arbitrary"),
    )(proj, proj, init8, conv_w, conv_b)


def _ssd_kernel(xc_ref, dtx_ref, z_ref, init_ref, alog_ref, dskip_ref, gssm_ref, y_ref, st_ref, *, lc):
    c = pl.program_id(1)

    @pl.when(c == 0)
    def _():
        st_ref[...] = init_ref[...]

    gw = SSM_D_INNER // SSM_GROUPS
    xs = xc_ref[:, 0:SSM_D_INNER]
    dt = dtx_ref[:, DT_LANE:DT_LANE + SSM_HEADS]
    a_head = -jnp.exp(alog_ref[...])
    adt = dt * a_head
    row = lax.broadcasted_iota(jnp.int32, (lc, lc), 0)
    colm = lax.broadcasted_iota(jnp.int32, (lc, lc), 1)
    tril = row >= colm
    a_cs = jnp.dot(tril.astype(F32), adt, preferred_element_type=F32, precision=HIGHEST)
    eye = (lax.broadcasted_iota(jnp.int32, (SSM_HEADS, SSM_HEADS), 0)
           == lax.broadcasted_iota(jnp.int32, (SSM_HEADS, SSM_HEADS), 1)).astype(F32)
    a_cs_t = _dot_nt(eye, a_cs, precision=HIGHEST)
    a_tot = a_cs[lc - 1:lc, :]
    spread = (lax.broadcasted_iota(jnp.int32, (SSM_HEADS, SSM_D_INNER), 1) // SSM_HEAD_DIM
              == lax.broadcasted_iota(jnp.int32, (SSM_HEADS, SSM_D_INNER), 0)).astype(F32)
    spread_t = (lax.broadcasted_iota(jnp.int32, (SSM_D_INNER, SSM_HEADS), 0) // SSM_HEAD_DIM
                == lax.broadcasted_iota(jnp.int32, (SSM_D_INNER, SSM_HEADS), 1)).astype(F32)

    def expand(v):
        return jnp.dot(v, spread, preferred_element_type=F32, precision=HIGHEST)

    xdt = xs * expand(dt)
    xd_state = (xdt * expand(jnp.exp(a_tot - a_cs))).astype(BF16)
    lane = lax.broadcasted_iota(jnp.int32, (lc, 128), 1)
    y_groups = []
    ds_groups = []
    for g in range(SSM_GROUPS):
        bg = xc_ref[:, SSM_D_INNER + g * SSM_STATE: SSM_D_INNER + (g + 1) * SSM_STATE].astype(BF16)
        cg = xc_ref[:, SSM_D_INNER + (SSM_GROUPS + g) * SSM_STATE:
                    SSM_D_INNER + (SSM_GROUPS + g + 1) * SSM_STATE].astype(BF16)
        cb = _dot_nt(cg, bg)
        s_g = st_ref[g * gw:(g + 1) * gw, :]
        y_off = _dot_nt(cg, s_g.astype(BF16))
        pairs = []
        for pr in range(gw // 128):
            x_pair = xdt[:, g * gw + pr * 128: g * gw + (pr + 1) * 128]
            acc = None
            for sub in range(2):
                h = (g * gw + pr * 128) // SSM_HEAD_DIM + sub
                seg = a_cs[:, h:h + 1] - a_cs_t[h:h + 1, :]
                decay = jnp.exp(jnp.where(tril, seg, -jnp.inf))
                gmat = (cb * decay).astype(BF16)
                x_h = jnp.where((lane // SSM_HEAD_DIM) == sub, x_pair, 0.0).astype(BF16)
                part = _dot(gmat, x_h)
                acc = part if acc is None else acc + part
            pairs.append(acc)
        y_groups.append((jnp.concatenate(pairs, axis=1), y_off))
        ds_groups.append(lax.dot_general(xd_state[:, g * gw:(g + 1) * gw], bg, TN_DIMS,
                                         preferred_element_type=F32))
    y_diag = jnp.concatenate([p[0] for p in y_groups], axis=1)
    y_off = jnp.concatenate([p[1] for p in y_groups], axis=1)
    y = y_diag + y_off * expand(jnp.exp(a_cs)) + expand(dskip_ref[...]) * xs
    tot_col = jnp.exp(a_cs_t[:, lc - 1:lc])
    decay_full = jnp.dot(spread_t, jnp.broadcast_to(tot_col, (SSM_HEADS, SSM_STATE)),
                         preferred_element_type=F32, precision=HIGHEST)
    st_ref[...] = decay_full * st_ref[...] + jnp.concatenate(ds_groups, axis=0)
    z = z_ref[...]
    v = y * _silu(z)
    outs = []
    for g in range(SSM_GROUPS):
        vg = v[:, g * gw:(g + 1) * gw]
        outs.append(vg * lax.rsqrt(jnp.mean(vg * vg, axis=-1, keepdims=True) + RMS_EPS))
    y_ref[...] = jnp.concatenate(outs, axis=1) * gssm_ref[...]


def _ssd(xconv, dtx, proj, init_state, layer, a_log, d_skip, g_ssm, seq_len):
    rows = xconv.shape[0]
    n_seq = rows // seq_len
    lc = min(SSM_CHUNK, seq_len)
    nc = seq_len // lc
    zblk = COL_Z // SSM_D_INNER
    vec = lambda a: pl.BlockSpec(a.shape, lambda b, c: (0, 0))
    return pl.pallas_call(
        functools.partial(_ssd_kernel, lc=lc),
        grid=(n_seq, nc),
        in_specs=[pl.BlockSpec((lc, CONV_DIM), lambda b, c: (b * nc + c, 0)),
                  pl.BlockSpec((lc, 128), lambda b, c: (b * nc + c, 0)),
                  pl.BlockSpec((lc, SSM_D_INNER), lambda b, c: (b * nc + c, zblk)),
                  pl.BlockSpec((None, None, SSM_D_INNER, SSM_STATE), lambda b, c: (layer, b, 0, 0)),
                  vec(a_log), vec(d_skip), vec(g_ssm)],
        out_specs=[pl.BlockSpec((lc, SSM_D_INNER), lambda b, c: (b * nc + c, 0)),
                   pl.BlockSpec((None, SSM_D_INNER, SSM_STATE), lambda b, c: (b, 0, 0))],
        out_shape=[jax.ShapeDtypeStruct((rows, SSM_D_INNER), F32),
                   jax.ShapeDtypeStruct((n_seq, SSM_D_INNER, SSM_STATE), F32)],
        compiler_params=_params("parallel", "arbitrary"),
    )(xconv, dtx, proj, init_state, a_log, d_skip, g_ssm)


def _softmax_init(m_scr, l_scr, acc_scr):
    m_scr[...] = jnp.full_like(m_scr, MASK_VALUE)
    l_scr[...] = jnp.zeros_like(l_scr)
    acc_scr[...] = jnp.zeros_like(acc_scr)


def _softmax_step(s, m_scr, l_scr, acc_scr, pv):
    m_prev = m_scr[...]
    m_new = jnp.maximum(m_prev, jnp.max(s, axis=1, keepdims=True))
    alpha = jnp.exp(m_prev - m_new)
    p = jnp.exp(s - m_new)
    l_scr[...] = alpha * l_scr[...] + jnp.sum(p, axis=1, keepdims=True)
    acc_scr[...] = alpha * acc_scr[...] + pv(p.astype(BF16))
    m_scr[...] = m_new


def _transpose_mxu(x):
    c = x.shape[1]
    eye = (lax.broadcasted_iota(jnp.int32, (c, c), 0) == lax.broadcasted_iota(jnp.int32, (c, c), 1))
    return _dot_nt(jnp.where(eye, 1.0, 0.0).astype(BF16), x).astype(BF16)


def _pad_rows(x, n):
    return jnp.concatenate([x, jnp.zeros((n - x.shape[0], x.shape[1]), x.dtype)], axis=0)


KV_UNROLL = 1
def _mla_prompt_kernel(ql_ref, qr_ref, ckv_ref, kr_ref, wuv_ref, o_ref,
                       ckv_b, kr_b, m_scr, l_scr, acc_scr, *, tq, tk):
    i = pl.program_id(1)

    @pl.when(i == 0)
    def _():
        ckv_b[...] = ckv_ref[...].astype(BF16)
        kr_b[...] = kr_ref[...].astype(BF16)

    rows = MLA_HEADS * tq
    q1 = ql_ref[...].reshape(rows, MLA_KV_RANK).astype(BF16)
    q2 = qr_ref[...].reshape(rows, MLA_ROPE_DIM).astype(BF16)
    _softmax_init(m_scr, l_scr, acc_scr)
    q_pos = i * tq + lax.broadcasted_iota(jnp.int32, (rows, tk), 0) % tq
    key_off = lax.broadcasted_iota(jnp.int32, (rows, tk), 1)

    def body(j, carry):
        for u in range(KV_UNROLL):
            ks = pl.multiple_of((j * KV_UNROLL + u) * tk, tk)
            kc = ckv_b[pl.ds(ks, tk), :]
            s = (_dot_nt(q1, kc) + _dot_nt(q2, kr_b[pl.ds(ks, tk), :])) * MLA_SCALE
            s = jnp.where(key_off + ks <= q_pos, s, MASK_VALUE)
            _softmax_step(s, m_scr, l_scr, acc_scr, lambda p: _dot(p, kc))
        return carry

    lax.fori_loop(0, (i * tq) // (tk * KV_UNROLL) + 1, body, 0)
    lat = (acc_scr[...] / l_scr[...]).astype(BF16)
    for h in range(MLA_HEADS):
        o_ref[:, h * MLA_V_DIM:(h + 1) * MLA_V_DIM] = _dot(lat[h * tq:(h + 1) * tq], wuv_ref[h])


def _mla_prompt(q_lat, q_rope, ckv, krope, w_uv_b, n_seq, seq_len):
    tq = 256
    tk = 1024
    assert seq_len % (tk * KV_UNROLL) == 0 and (tk * KV_UNROLL) % tq == 0
    nt = seq_len // tq
    rows = n_seq * seq_len
    return pl.pallas_call(
        functools.partial(_mla_prompt_kernel, tq=tq, tk=tk),
        grid=(n_seq, nt),
        in_specs=[pl.BlockSpec((MLA_HEADS, tq, MLA_KV_RANK), lambda b, i: (0, b * nt + i, 0)),
                  pl.BlockSpec((MLA_HEADS, tq, MLA_ROPE_DIM), lambda b, i: (0, b * nt + i, 0)),
                  pl.BlockSpec((seq_len, MLA_KV_RANK), lambda b, i: (b, 0)),
                  pl.BlockSpec((seq_len, MLA_ROPE_DIM), lambda b, i: (b, 0)),
                  pl.BlockSpec(w_uv_b.shape, lambda b, i: (0, 0, 0))],
        out_specs=pl.BlockSpec((tq, MLA_HEADS * MLA_V_DIM), lambda b, i: (b * nt + i, 0)),
        out_shape=jax.ShapeDtypeStruct((rows, MLA_HEADS * MLA_V_DIM), F32),
        scratch_shapes=[pltpu.VMEM((seq_len, MLA_KV_RANK), BF16), pltpu.VMEM((seq_len, MLA_ROPE_DIM), BF16),
                        pltpu.VMEM((MLA_HEADS * tq, 1), F32), pltpu.VMEM((MLA_HEADS * tq, 1), F32),
                        pltpu.VMEM((MLA_HEADS * tq, MLA_KV_RANK), F32)],
        compiler_params=_params("parallel", "arbitrary"),
    )(q_lat, q_rope, ckv, krope, w_uv_b)


def _moba_select(gate, n_valid, block_axis):
    nb = gate.shape[block_axis]
    blk = lax.broadcasted_iota(jnp.int32, gate.shape, block_axis)
    valid = blk < n_valid
    g = jnp.where(valid, gate, -jnp.inf)
    rank = jnp.zeros(gate.shape, jnp.int32)
    for m in range(nb):
        gm = lax.slice_in_dim(g, m, m + 1, axis=block_axis)
        ahead = (gm > g) | ((gm == g) & (m < blk))
        rank = rank + jnp.where(ahead, 1, 0)
    return jnp.where(valid & (rank < MOBA_TOPK), 1.0, 0.0).astype(F32)


def _moba_prompt_kernel(q_ref, k_ref, v_ref, o_ref, k_b, v_b, kmean_scr, m_scr, l_scr, acc_scr, *, nb):
    i = pl.program_id(1)
    tq = MOBA_BLOCK
    rows = MOBA_HEADS * tq

    @pl.when(i == 0)
    def _():
        k = k_ref[...]
        k_b[...] = k.astype(BF16)
        v_b[...] = v_ref[...].astype(BF16)
        kmean_scr[...] = jnp.zeros_like(kmean_scr)
        kmean_scr[0:nb, :] = jnp.mean(k.reshape(nb, MOBA_BLOCK, MOBA_HEAD_DIM), axis=1)

    q = q_ref[...].reshape(rows, MOBA_HEAD_DIM)
    qb = q.astype(BF16)
    gate_t = _dot_nt(kmean_scr[...], q, precision=HIGHEST)
    sel_t = _moba_select(gate_t[0:nb], i, 0)
    sel = jnp.concatenate([sel_t, jnp.zeros((128 - nb, rows), F32)], axis=0).T
    blk_lane = lax.broadcasted_iota(jnp.int32, (rows, 128), 1)
    _softmax_init(m_scr, l_scr, acc_scr)

    def attend(j, n_blocks, mask_fn):
        ks = pl.multiple_of(j * MOBA_BLOCK, MOBA_BLOCK)
        s = _dot_nt(qb, k_b[pl.ds(ks, n_blocks * MOBA_BLOCK), :]) * MOBA_SCALE
        s = jnp.where(mask_fn(), s, MASK_VALUE)
        _softmax_step(s, m_scr, l_scr, acc_scr, lambda p: _dot(p, v_b[pl.ds(ks, n_blocks * MOBA_BLOCK), :]))

    t_q = lax.broadcasted_iota(jnp.int32, (rows, MOBA_BLOCK), 0) % tq
    t_k = lax.broadcasted_iota(jnp.int32, (rows, MOBA_BLOCK), 1)
    attend(i, 1, lambda: t_k <= t_q)

    def picked(j):
        col = jnp.sum(jnp.where(blk_lane == j, sel, 0.0), axis=1, keepdims=True) > 0.5
        return jnp.broadcast_to(col, (rows, MOBA_BLOCK))

    def body(j, carry):
        attend(2 * j, 2, lambda: jnp.concatenate([picked(2 * j), picked(2 * j + 1)], axis=1))
        return carry

    lax.fori_loop(0, (i + 1) // 2, body, 0)
    out = acc_scr[...] / l_scr[...]
    for h in range(MOBA_HEADS):
        o_ref[:, h * MOBA_HEAD_DIM:(h + 1) * MOBA_HEAD_DIM] = out[h * tq:(h + 1) * tq]


def _moba_prompt(qm, km, proj, n_seq, seq_len):
    nb = seq_len // MOBA_BLOCK
    rows = n_seq * seq_len
    vblk = COL_VM // MOBA_HEAD_DIM
    return pl.pallas_call(
        functools.partial(_moba_prompt_kernel, nb=nb),
        grid=(n_seq, nb),
        in_specs=[pl.BlockSpec((MOBA_HEADS, MOBA_BLOCK, MOBA_HEAD_DIM), lambda b, i: (0, b * nb + i, 0)),
                  pl.BlockSpec((seq_len, MOBA_HEAD_DIM), lambda b, i: (b, 0)),
                  pl.BlockSpec((seq_len, MOBA_HEAD_DIM), lambda b, i: (b, vblk))],
        out_specs=pl.BlockSpec((MOBA_BLOCK, MOBA_HEADS * MOBA_HEAD_DIM), lambda b, i: (b * nb + i, 0)),
        out_shape=jax.ShapeDtypeStruct((rows, MOBA_HEADS * MOBA_HEAD_DIM), F32),
        scratch_shapes=[pltpu.VMEM((seq_len, MOBA_HEAD_DIM), BF16), pltpu.VMEM((seq_len, MOBA_HEAD_DIM), BF16),
                        pltpu.VMEM((128, MOBA_HEAD_DIM), F32),
                        pltpu.VMEM((MOBA_HEADS * MOBA_BLOCK, 1), F32), pltpu.VMEM((MOBA_HEADS * MOBA_BLOCK, 1), F32),
                        pltpu.VMEM((MOBA_HEADS * MOBA_BLOCK, MOBA_HEAD_DIM), F32)],
        compiler_params=_params("parallel", "arbitrary"),
    )(qm, km, proj)


SAMPLE_CHUNKS = 2
SCORE_SPLIT = 2


def _attn_sample_kernel(pt_ref, ckv_pool, kr_pool, mk_pool, mv_pool,
                        ql_ref, qr_ref, qm_ref, cnew_ref, knew_ref, kmnew_ref, vmnew_ref, wuv_ref,
                        mla_o_ref, moba_o_ref,
                        ckv_buf, kr_buf, mk_buf, mv_buf, kstash, kmean_scr, selmask,
                        m1, l1, acc1, m2, l2, acc2, sems, *, layer, t_new, ch):
    seq = pl.program_id(0)
    n_seq = pl.num_programs(0)
    rows = MLA_HEADS * t_new
    ckeys = ch * PAGE_SIZE
    nb_chunk = ckeys // MOBA_BLOCK

    def copies_a(page_of, slot):
        out = []
        for j in range(ch):
            page, win = page_of(j), pl.ds(j * PAGE_SIZE, PAGE_SIZE)
            out.append(pltpu.make_async_copy(ckv_pool.at[layer, page], ckv_buf.at[slot, win, :], sems.at[0, slot]))
            out.append(pltpu.make_async_copy(kr_pool.at[layer, page], kr_buf.at[slot, :, win], sems.at[1, slot]))
            out.append(pltpu.make_async_copy(mk_pool.at[layer, page], mk_buf.at[slot, win, :], sems.at[2, slot]))
        return out

    def copies_b(page_of, slot):
        return [pltpu.make_async_copy(mv_pool.at[layer, page_of(j)],
                                      mv_buf.at[slot, pl.ds(j * PAGE_SIZE, PAGE_SIZE), :], sems.at[3, slot])
                for j in range(ch)]

    def start(copies, sq, k):
        for c in copies(lambda j: pt_ref[sq, k * ch + j], k % 2):
            c.start()

    def wait(copies, k):
        for c in copies(lambda j: 0, k % 2):
            c.wait()

    def refill(copies, k):
        if k + 2 < SAMPLE_CHUNKS:
            start(copies, seq, k + 2)
        else:
            @pl.when(seq + 1 < n_seq)
            def _():
                start(copies, seq + 1, k + 2 - SAMPLE_CHUNKS)

    @pl.when(seq == 0)
    def _():
        for k in range(2):
            start(copies_a, seq, k)
            start(copies_b, seq, k)

    t_q = lax.broadcasted_iota(jnp.int32, (rows, t_new), 0) % t_new
    t_k = lax.broadcasted_iota(jnp.int32, (rows, t_new), 1)
    causal_new = t_k <= t_q

    def scores(q_rows, keys):
        n = keys.shape[0] // SCORE_SPLIT
        return jnp.concatenate([_dot_nt(q_rows, keys[u * n:(u + 1) * n]) for u in range(SCORE_SPLIT)], axis=1)

    q1 = ql_ref[...].reshape(rows, MLA_KV_RANK).astype(BF16)
    q2 = qr_ref[...].reshape(rows, MLA_ROPE_DIM).astype(BF16)
    _softmax_init(m1, l1, acc1)
    for k in range(SAMPLE_CHUNKS):
        slot = k % 2
        wait(copies_a, k)
        kc = ckv_buf[slot].astype(BF16)
        s = (scores(q1, kc) + _dot(q2, kr_buf[slot].astype(BF16))) * MLA_SCALE
        _softmax_step(s, m1, l1, acc1, lambda p: _dot(p, kc))
        mk = mk_buf[slot]
        kstash[k * ckeys:(k + 1) * ckeys, :] = mk.astype(BF16)
        kmean_scr[k * nb_chunk:(k + 1) * nb_chunk, :] = jnp.mean(
            mk.reshape(nb_chunk, MOBA_BLOCK, MOBA_HEAD_DIM), axis=1)
        refill(copies_a, k)

    cn = cnew_ref[...].astype(BF16)
    sn = (_dot_nt(q1, cn) + _dot_nt(q2, knew_ref[...].astype(BF16))) * MLA_SCALE
    _softmax_step(jnp.where(causal_new, sn, MASK_VALUE), m1, l1, acc1, lambda p: _dot(p, cn))
    lat = (acc1[...] / l1[...]).astype(BF16)
    for h in range(MLA_HEADS):
        mla_o_ref[:, h * MLA_V_DIM:(h + 1) * MLA_V_DIM] = _dot(lat[h * t_new:(h + 1) * t_new], wuv_ref[h])

    q = qm_ref[...].reshape(rows, MOBA_HEAD_DIM)
    qb = q.astype(BF16)
    _softmax_init(m2, l2, acc2)
    vn = vmnew_ref[...].astype(BF16)
    sm = _dot_nt(qb, kmnew_ref[...].astype(BF16)) * MOBA_SCALE
    _softmax_step(jnp.where(causal_new, sm, MASK_VALUE), m2, l2, acc2, lambda p: _dot(p, vn))
    nb_past = SAMPLE_CHUNKS * nb_chunk
    gate = _dot_nt(q, kmean_scr[...], precision=HIGHEST)
    sel = _moba_select(gate, nb_past, 1)
    for b in range(nb_past):
        selmask[:, b * MOBA_BLOCK:(b + 1) * MOBA_BLOCK] = jnp.broadcast_to(sel[:, b:b + 1], (rows, MOBA_BLOCK))

    for k in range(SAMPLE_CHUNKS):
        slot = k % 2
        wait(copies_b, k)
        v = mv_buf[slot].astype(BF16)
        s = scores(qb, kstash[k * ckeys:(k + 1) * ckeys, :]) * MOBA_SCALE
        s = jnp.where(selmask[:, k * ckeys:(k + 1) * ckeys] > 0.5, s, MASK_VALUE)
        _softmax_step(s, m2, l2, acc2, lambda p: _dot(p, v))
        refill(copies_b, k)

    out = acc2[...] / l2[...]
    for h in range(MOBA_HEADS):
        moba_o_ref[:, h * MOBA_HEAD_DIM:(h + 1) * MOBA_HEAD_DIM] = out[h * t_new:(h + 1) * t_new]


def _attn_sample(page_table, ckv_pool, kr_pool_t, mk_pool, mv_pool, layer,
                 q_lat, q_rope, qm, ckv_new, kr_new, km_new, proj, w_uv_b, t_new):
    n_seq, n_pages = page_table.shape
    assert n_pages % (SAMPLE_CHUNKS * MOBA_BLOCK // PAGE_SIZE) == 0
    ch = n_pages // SAMPLE_CHUNKS
    ckeys = ch * PAGE_SIZE
    past_len = n_pages * PAGE_SIZE
    rows = MLA_HEADS * t_new
    vblk = COL_VM // MOBA_HEAD_DIM
    hbm = pl.BlockSpec(memory_space=pl.ANY)
    heads = lambda w: pl.BlockSpec((MLA_HEADS, t_new, w), lambda s, pt: (0, s, 0))
    new_rows = lambda w, blk=0: pl.BlockSpec((t_new, w), lambda s, pt: (s, blk))
    grid_spec = pltpu.PrefetchScalarGridSpec(
        num_scalar_prefetch=1, grid=(n_seq,),
        in_specs=[hbm, hbm, hbm, hbm,
                  heads(MLA_KV_RANK), heads(MLA_ROPE_DIM), heads(MOBA_HEAD_DIM),
                  new_rows(MLA_KV_RANK), new_rows(MLA_ROPE_DIM), new_rows(MOBA_HEAD_DIM),
                  new_rows(MOBA_HEAD_DIM, vblk),
                  pl.BlockSpec(w_uv_b.shape, lambda s, pt: (0, 0, 0))],
        out_specs=[new_rows(MLA_HEADS * MLA_V_DIM), new_rows(MOBA_HEADS * MOBA_HEAD_DIM)],
        scratch_shapes=[pltpu.VMEM((2, ckeys, MLA_KV_RANK), F32),
                        pltpu.VMEM((2, MLA_ROPE_DIM, ckeys), F32),
                        pltpu.VMEM((2, ckeys, MOBA_HEAD_DIM), F32),
                        pltpu.VMEM((2, ckeys, MOBA_HEAD_DIM), F32),
                        pltpu.VMEM((past_len, MOBA_HEAD_DIM), BF16),
                        pltpu.VMEM((past_len // MOBA_BLOCK, MOBA_HEAD_DIM), F32),
                        pltpu.VMEM((rows, past_len), F32),
                        pltpu.VMEM((rows, 1), F32), pltpu.VMEM((rows, 1), F32),
                        pltpu.VMEM((rows, MLA_KV_RANK), F32),
                        pltpu.VMEM((rows, 1), F32), pltpu.VMEM((rows, 1), F32),
                        pltpu.VMEM((rows, MOBA_HEAD_DIM), F32),
                        pltpu.SemaphoreType.DMA((4, 2))])
    return pl.pallas_call(
        functools.partial(_attn_sample_kernel, layer=layer, t_new=t_new, ch=ch),
        grid_spec=grid_spec,
        out_shape=[jax.ShapeDtypeStruct((n_seq * t_new, MLA_HEADS * MLA_V_DIM), F32),
                   jax.ShapeDtypeStruct((n_seq * t_new, MOBA_HEADS * MOBA_HEAD_DIM), F32)],
        compiler_params=_params("arbitrary"),
    )(page_table, ckv_pool, kr_pool_t, mk_pool, mv_pool, q_lat, q_rope, qm, ckv_new, kr_new, km_new, proj, w_uv_b)


def _outproj_kernel(mla_ref, moba_ref, ssm_ref, w_ref, x_ref, gt_ref, lg_ref, lb_ref, o_ref, *, alpha):
    w_mla = MLA_HEADS * MLA_V_DIM
    w_moba = MOBA_HEADS * MOBA_HEAD_DIM
    mixed = (_dot(mla_ref[...].astype(BF16), w_ref[0:w_mla, :])
             + _dot(moba_ref[...].astype(BF16), w_ref[w_mla:w_mla + w_moba, :])
             + _dot(ssm_ref[...].astype(BF16), w_ref[w_mla + w_moba:, :]))
    v = alpha * x_ref[...] + _mod_rows(gt_ref, x_ref.shape[0]) * mixed
    o_ref[...] = _layer_norm_rows(v, lg_ref[...], lb_ref[...])


def _outproj(mla_out, moba_out, ssm_out, w_out_b, layer, x, mod, ln_g, ln_b, alpha, tm):
    rows = x.shape[0]
    row = lambda w: pl.BlockSpec((tm, w), lambda i: (i, 0))
    vec_spec = pl.BlockSpec((1, D_MODEL), lambda i: (0, 0))
    return pl.pallas_call(
        functools.partial(_outproj_kernel, alpha=alpha),
        grid=(rows // tm,),
        in_specs=[row(mla_out.shape[1]), row(moba_out.shape[1]), row(ssm_out.shape[1]),
                  pl.BlockSpec((None, D_MODEL, D_MODEL), lambda i: (layer, 0, 0)),
                  row(D_MODEL), mod.spec(tm), vec_spec, vec_spec],
        out_specs=row(D_MODEL),
        out_shape=jax.ShapeDtypeStruct((rows, D_MODEL), F32),
        compiler_params=_params("parallel"),
    )(mla_out, moba_out, ssm_out, w_out_b, x, mod.arrays[2], ln_g, ln_b)


def _trunk(x, mods_by_layer, seq_len, tabs, conv_init, ssm_init, attn, w, tm):
    depth = w["w_in_p"].shape[0]
    alpha = (2 * depth) ** 0.25
    n_seq = x.shape[0] // seq_len
    tab_tiles = tabs[0].shape[0] // tm
    states = []
    for l in range(depth):
        mods = mods_by_layer[l]
        ln_g, ln_b = w["ln_g"][l], w["ln_b"][l]
        x = _ffn(x, mods[0], w["wg"], w["wu"], w["wd"], ln_g[0:1], ln_b[0:1], l, 0, alpha, tm)
        proj = _inproj(x, mods[1], w["w_in_p"], l, tm)
        q_lat, q_rope, ckv, krope, qm, km, dtx = _prep(
            proj, tabs, tab_tiles, w["g_q"][l:l + 1], w["w_uq_p"][l], w["g_kv"][l:l + 1], w["w_uk_t"][l],
            w["dt_bias_p"][l:l + 1], tm)
        conv_tm = min(tm, seq_len)
        init8 = jnp.pad(conv_init[l], ((0, 0), (8 - (CONV_WIDTH - 1), 0), (0, 0)))
        xconv = _conv(proj, init8, w["conv_w"][l], w["conv_b"][l:l + 1], seq_len, conv_tm)
        ssm_out, ssm_state = _ssd(xconv, dtx, proj, ssm_init, l, w["a_log"][l:l + 1], w["d_skip"][l:l + 1],
                                  w["g_ssm"][l:l + 1], seq_len)
        mla_out, moba_out = attn(l, q_lat, q_rope, ckv, krope, qm, km, proj)
        x = _outproj(mla_out, moba_out, ssm_out, w["w_out_b"], l, x, mods[1], ln_g[1:2], ln_b[1:2], alpha, tm)
        x = _ffn(x, mods[2], w["wg"], w["wu"], w["wd"], ln_g[2:3], ln_b[2:3], l, 1, alpha, tm)
        vm = proj[:, COL_VM:COL_VM + MOBA_HEAD_DIM]
        keep = CONV_WIDTH - 1
        xbc_tail = proj.reshape(n_seq, seq_len, IN_COLS_PADDED)[:, seq_len - min(keep, seq_len):,
                                                                COL_XBC:COL_XBC + CONV_DIM]
        conv_state = jnp.concatenate([conv_init[l], xbc_tail], axis=1)[:, -keep:]
        states.append((ckv.reshape(n_seq, seq_len, MLA_KV_RANK),
                       krope.reshape(n_seq, seq_len, MLA_ROPE_DIM),
                       km.reshape(n_seq, seq_len, 1, MOBA_HEAD_DIM),
                       vm.reshape(n_seq, seq_len, 1, MOBA_HEAD_DIM),
                       ssm_state.reshape(n_seq, SSM_HEADS, SSM_HEAD_DIM, SSM_STATE),
                       conv_state))
    return x, [jnp.stack([s[i] for s in states]) for i in range(len(states[0]))]


def _row_tile(rows, seq_len, largest):
    for tm in (512, 256, 128, 64, 32, 16, 8):
        if tm <= largest and rows % tm == 0 and (seq_len % tm == 0 or tm % seq_len == 0):
            return tm
    raise ValueError("row count must be a multiple of 8")


def kernel(x_prompt, x_sample, cache_mla_ckv, cache_mla_krope, cache_moba_k, cache_moba_v, state_ssm, state_conv,
           page_table, c_prompt, c_sample, w_ada, b_ada, ln_g, ln_b, ffn_w_gate, ffn_w_up, ffn_w_down,
           w_in, g_q, w_uq, g_kv, w_uk, w_uv, conv_w, conv_b, dt_bias, a_log, d_skip, g_ssm, w_out):
    depth = w_in.shape[0]
    n_prompt, seq_len, _ = x_prompt.shape
    n_sample, t_new, _ = x_sample.shape
    n_pages = page_table.shape[1]
    past_len = n_pages * PAGE_SIZE
    n_pool = cache_mla_ckv.shape[1]

    assert past_len % MOBA_BLOCK == 0 and t_new <= MOBA_BLOCK and seq_len % MOBA_BLOCK == 0
    names = ["cq", "ckv", "kr", "qm", "km", "vm", "z", "xbc", "dt"]
    widths = [MLA_Q_RANK, MLA_KV_RANK, MLA_ROPE_DIM, MOBA_HEADS * MOBA_HEAD_DIM, MOBA_HEAD_DIM, MOBA_HEAD_DIM,
              SSM_D_INNER, CONV_DIM, SSM_HEADS]
    pieces, start = {}, 0
    for name, width in zip(names, widths):
        pieces[name] = (start, start + width)
        start += width
    order = ["cq", "qm", "ckv", "km", "vm", "xbc", "z", "kr", "dt"]
    w_in_p = jnp.concatenate([w_in[:, :, pieces[k][0]:pieces[k][1]] for k in order]
                             + [jnp.zeros((depth, D_MODEL, IN_COLS_PADDED - w_in.shape[2]), w_in.dtype)],
                             axis=2).astype(BF16)
    qk = MLA_NOPE_DIM + MLA_ROPE_DIM
    w_uq_h = w_uq.reshape(depth, MLA_Q_RANK, MLA_HEADS, qk)
    w_uq_p = jnp.concatenate([w_uq_h[..., :MLA_NOPE_DIM].reshape(depth, MLA_Q_RANK, MLA_HEADS * MLA_NOPE_DIM),
                              w_uq_h[..., MLA_NOPE_DIM:].reshape(depth, MLA_Q_RANK, MLA_HEADS * MLA_ROPE_DIM)],
                             axis=2).astype(BF16)
    dt_bias_p = jnp.zeros((depth, 128), F32).at[:, DT_LANE:DT_LANE + SSM_HEADS].set(dt_bias)
    w = dict(
        wg=ffn_w_gate.astype(BF16), wu=ffn_w_up.astype(BF16), wd=ffn_w_down.astype(BF16),
        w_in_p=w_in_p, w_uq_p=w_uq_p, w_uk_t=jnp.swapaxes(w_uk, 2, 3).astype(BF16), w_uv_b=w_uv.astype(BF16),
        w_out_b=w_out.astype(BF16), g_q=g_q, g_kv=g_kv, ln_g=ln_g, ln_b=ln_b,
        conv_w=conv_w.reshape(depth, CONV_WIDTH, CONV_DIM), conv_b=conv_b, dt_bias_p=dt_bias_p,
        a_log=a_log, d_skip=d_skip, g_ssm=g_ssm)

    n_c = n_prompt + n_sample
    c_rows = -(-n_c // 8) * 8
    c_all = jnp.concatenate([c_prompt, c_sample, jnp.zeros((c_rows - n_c, D_MODEL), F32)], axis=0)
    mod = _ada(c_all, w_ada, b_ada)
    mods_p = [_make_mods(mod[l, :n_prompt], seq_len) for l in range(depth)]
    mods_s = [_make_mods(mod[l, n_prompt:n_c], t_new) for l in range(depth)]

    tabs_p = _rope_tables(jnp.arange(seq_len, dtype=jnp.int32))
    conv0 = jnp.zeros((depth, n_prompt, CONV_WIDTH - 1, CONV_DIM), F32)
    ssm0 = jnp.zeros((depth, n_prompt, SSM_D_INNER, SSM_STATE), F32)

    def attn_prompt(l, q_lat, q_rope, ckv, krope, qm, km, proj):
        return (_mla_prompt(q_lat, q_rope, ckv, krope, w["w_uv_b"][l], n_prompt, seq_len),
                _moba_prompt(qm, km, proj, n_prompt, seq_len))

    tm_p = _row_tile(n_prompt * seq_len, seq_len, 512)
    y_p, st_p = _trunk(x_prompt.reshape(n_prompt * seq_len, D_MODEL), mods_p, seq_len, tabs_p, conv0, ssm0,
                       attn_prompt, w, tm_p)

    pos_s = past_len + jnp.arange(t_new, dtype=jnp.int32)
    tabs_s = tuple(jnp.tile(t, (n_sample, 1)) for t in _rope_tables(pos_s))
    mk_pool = cache_moba_k.reshape(depth, n_pool, PAGE_SIZE, MOBA_HEAD_DIM)
    mv_pool = cache_moba_v.reshape(depth, n_pool, PAGE_SIZE, MOBA_HEAD_DIM)

    kr_pool_t = jnp.swapaxes(cache_mla_krope, 2, 3)

    def attn_sample(l, q_lat, q_rope, ckv, krope, qm, km, proj):
        return _attn_sample(page_table, cache_mla_ckv, kr_pool_t, mk_pool, mv_pool, l,
                            q_lat, q_rope, qm, ckv, krope, km, proj, w["w_uv_b"][l], t_new)

    tm_s = _row_tile(n_sample * t_new, t_new, 512)
    ssm_init_s = state_ssm.reshape(depth, n_sample, SSM_D_INNER, SSM_STATE)
    y_s, st_s = _trunk(x_sample.reshape(n_sample * t_new, D_MODEL), mods_s, t_new, tabs_s, state_conv, ssm_init_s,
                       attn_sample, w, tm_s)

    return (y_p.reshape(n_prompt, seq_len, D_MODEL), y_s.reshape(n_sample, t_new, D_MODEL),
            *st_p, *st_s)
```

```python
import functools
import math

import jax
import jax.numpy as jnp
from jax import lax
from jax.experimental import pallas as pl
from jax.experimental.pallas import tpu as pltpu

F32 = jnp.float32
BF16 = jnp.bfloat16

D_MODEL = 2048
PAGE_SIZE = 128
MLA_HEADS = 4
MLA_Q_RANK = 512
MLA_KV_RANK = 256
MLA_NOPE_DIM = 128
MLA_ROPE_DIM = 64
MLA_V_DIM = 128
MLA_SCALE = (MLA_NOPE_DIM + MLA_ROPE_DIM) ** -0.5
MOBA_HEADS = 4
MOBA_HEAD_DIM = 128
MOBA_BLOCK = 256
MOBA_TOPK = 3
MOBA_SCALE = MOBA_HEAD_DIM ** -0.5
SSM_HEADS = 16
SSM_HEAD_DIM = 64
SSM_D_INNER = SSM_HEADS * SSM_HEAD_DIM
SSM_GROUPS = 2
SSM_STATE = 128
SSM_CHUNK = 128
CONV_WIDTH = 4
CONV_DIM = SSM_D_INNER + 2 * SSM_GROUPS * SSM_STATE
MACARON_WEIGHT = 0.5
ROPE_THETA = 10000.0
LN_EPS = 1e-5
RMS_EPS = 1e-6

COL_CQ = 0
COL_QM = 512
COL_CKV = 1024
COL_KM = 1280
COL_VM = 1408
COL_XBC = 1536
COL_Z = 3072
COL_KRDT = 4096
IN_COLS_PADDED = 4224
DT_LANE = 64

MASK_VALUE = -1e30
VMEM_LIMIT = 56 * 1024 * 1024

HIGHEST = lax.Precision.HIGHEST
NT_DIMS = (((1,), (1,)), ((), ()))
TN_DIMS = (((0,), (0,)), ((), ()))


def _params(*sem):
    return pltpu.CompilerParams(dimension_semantics=sem, vmem_limit_bytes=VMEM_LIMIT)


def _silu(v):
    return v * jax.nn.sigmoid(v)


def _dot(a, b):
    return jnp.dot(a, b, preferred_element_type=F32)


def _dot_nt(a, b, precision=None):
    return lax.dot_general(a, b, NT_DIMS, preferred_element_type=F32, precision=precision)


def _layer_norm_rows(v, g, b):
    mu = jnp.mean(v, axis=-1, keepdims=True)
    d = v - mu
    var = jnp.mean(d * d, axis=-1, keepdims=True)
    return d * lax.rsqrt(var + LN_EPS) * g + b


def _ada_kernel(c_ref, w_ref, b_ref, o_ref):
    h = _silu(c_ref[...]).astype(BF16)
    o_ref[...] = _dot(h, w_ref[...].astype(BF16)) + b_ref[...]


def _ada(c_all, w_ada, b_ada):
    depth, d, n = w_ada.shape
    rows = c_all.shape[0]
    tn = 1024
    return pl.pallas_call(
        _ada_kernel,
        grid=(depth, n // tn),
        in_specs=[
            pl.BlockSpec((rows, d), lambda l, j: (0, 0)),
            pl.BlockSpec((None, d, tn), lambda l, j: (l, 0, j)),
            pl.BlockSpec((None, 1, tn), lambda l, j: (l, 0, j)),
        ],
        out_specs=pl.BlockSpec((None, rows, tn), lambda l, j: (l, 0, j)),
        out_shape=jax.ShapeDtypeStruct((depth, rows, n), F32),
        compiler_params=_params("parallel", "parallel"),
    )(c_all, w_ada, b_ada.reshape(depth, 1, n))


class _Mod:
    def __init__(self, arrays, seq_len):
        self.arrays = arrays
        self.seq_len = seq_len

    def spec(self, tm):
        if self.seq_len >= tm:
            tps = self.seq_len // tm
            return pl.BlockSpec((None, 1, D_MODEL), lambda i, *_: (i // tps, 0, 0))
        return pl.BlockSpec((tm // self.seq_len, 1, D_MODEL), lambda i, *_: (i, 0, 0))


def _mod_rows(m_ref, tm):
    m = m_ref[...]
    if m.ndim == 2:
        return m
    g = m.shape[0]
    return jnp.broadcast_to(m, (g, tm // g, D_MODEL)).reshape(tm, D_MODEL)


def _make_mods(mod, seq_len):
    n_seq = mod.shape[0]
    m = mod.reshape(n_seq, 3, 3, 1, D_MODEL)
    return [_Mod([m[:, j, k] for k in range(3)], seq_len) for j in range(3)]


def _ffn_kernel(x_ref, sh_ref, sc_ref, gt_ref, wg_ref, wu_ref, wd_ref, lg_ref, lb_ref, o_ref,
                h_scr, acc_scr, *, alpha, weight):
    j = pl.program_id(1)
    tm = x_ref.shape[0]

    @pl.when(j == 0)
    def _():
        h_scr[...] = (x_ref[...] * (1.0 + _mod_rows(sc_ref, tm)) + _mod_rows(sh_ref, tm)).astype(BF16)
        acc_scr[...] = jnp.zeros_like(acc_scr)

    h = h_scr[...]
    a = _dot(h, wg_ref[...])
    u = _dot(h, wu_ref[...])
    act = (_silu(a) * u).astype(BF16)
    acc_scr[...] += _dot(act, wd_ref[...])

    @pl.when(j == pl.num_programs(1) - 1)
    def _():
        v = alpha * x_ref[...] + weight * _mod_rows(gt_ref, tm) * acc_scr[...]
        o_ref[...] = _layer_norm_rows(v, lg_ref[...], lb_ref[...])


def _ffn(x, mod, wg, wu, wd, ln_g, ln_b, layer, slot, alpha, tm):
    rows = x.shape[0]
    d_ff = wg.shape[-1]
    tf = 512 if d_ff % 512 == 0 else d_ff
    row_spec = pl.BlockSpec((tm, D_MODEL), lambda i, j: (i, 0))
    vec_spec = pl.BlockSpec((1, D_MODEL), lambda i, j: (0, 0))
    return pl.pallas_call(
        functools.partial(_ffn_kernel, alpha=alpha, weight=MACARON_WEIGHT),
        grid=(rows // tm, d_ff // tf),
        in_specs=[row_spec, mod.spec(tm), mod.spec(tm), mod.spec(tm),
                  pl.BlockSpec((None, None, D_MODEL, tf), lambda i, j: (layer, slot, 0, j)),
                  pl.BlockSpec((None, None, D_MODEL, tf), lambda i, j: (layer, slot, 0, j)),
                  pl.BlockSpec((None, None, tf, D_MODEL), lambda i, j: (layer, slot, j, 0)),
                  vec_spec, vec_spec],
        out_specs=row_spec,
        out_shape=jax.ShapeDtypeStruct((rows, D_MODEL), F32),
        scratch_shapes=[pltpu.VMEM((tm, D_MODEL), BF16), pltpu.VMEM((tm, D_MODEL), F32)],
        compiler_params=_params("parallel", "arbitrary"),
    )(x, mod.arrays[0], mod.arrays[1], mod.arrays[2], wg, wu, wd, ln_g, ln_b)


def _inproj_kernel(x_ref, sh_ref, sc_ref, w_ref, o_ref):
    tm = x_ref.shape[0]
    h = (x_ref[...] * (1.0 + _mod_rows(sc_ref, tm)) + _mod_rows(sh_ref, tm)).astype(BF16)
    o_ref[...] = _dot(h, w_ref[...])


def _inproj(x, mod, w_in_p, layer, tm):
    rows = x.shape[0]
    tn = IN_COLS_PADDED // 3
    return pl.pallas_call(
        _inproj_kernel,
        grid=(rows // tm, 3),
        in_specs=[pl.BlockSpec((tm, D_MODEL), lambda i, j: (i, 0)), mod.spec(tm), mod.spec(tm),
                  pl.BlockSpec((None, D_MODEL, tn), lambda i, j: (layer, 0, j))],
        out_specs=pl.BlockSpec((tm, tn), lambda i, j: (i, j)),
        out_shape=jax.ShapeDtypeStruct((rows, IN_COLS_PADDED), F32),
        compiler_params=_params("parallel", "arbitrary"),
    )(x, mod.arrays[0], mod.arrays[1], w_in_p)


def _rope64(v, cos, sin):
    lane = lax.broadcasted_iota(jnp.int32, v.shape, 1)
    swap = jnp.where(lane % 64 < 32, pltpu.roll(v, 96, axis=1), pltpu.roll(v, 32, axis=1))
    return v * cos + swap * sin


def _rope128(v, cos, sin):
    return v * cos + pltpu.roll(v, 64, axis=1) * sin


def _rms(v, g):
    return v * lax.rsqrt(jnp.mean(v * v, axis=-1, keepdims=True) + RMS_EPS) * g


def _prep_kernel(cq_ref, qm_ref, ckvr_ref, km_ref, krdt_ref, cosa_ref, sina_ref, cosb_ref, sinb_ref,
                 gq_ref, wuq_ref, gkv_ref, wuk_ref, dtb_ref,
                 qlat_ref, qrope_ref, ckv_ref, krope_ref, qmo_ref, kmo_ref, dtx_ref):
    cosa, sina = cosa_ref[...], sina_ref[...]
    cosb, sinb = cosb_ref[...], sinb_ref[...]
    cqn = _rms(cq_ref[...], gq_ref[...]).astype(BF16)
    q = _dot(cqn, wuq_ref[...])
    nope_w = MLA_HEADS * MLA_NOPE_DIM
    for pair in range(MLA_HEADS // 2):
        r = _rope64(q[:, nope_w + pair * 128: nope_w + (pair + 1) * 128], cosa, sina)
        qrope_ref[2 * pair] = r[:, :MLA_ROPE_DIM]
        qrope_ref[2 * pair + 1] = r[:, MLA_ROPE_DIM:]
    for h in range(MLA_HEADS):
        qn = q[:, h * MLA_NOPE_DIM:(h + 1) * MLA_NOPE_DIM].astype(BF16)
        qlat_ref[h] = _dot(qn, wuk_ref[h])
    ckv_ref[...] = _rms(ckvr_ref[...], gkv_ref[...])
    krdt = krdt_ref[...]
    krope_ref[...] = _rope64(krdt, cosa, sina)[:, :MLA_ROPE_DIM]
    dtx_ref[...] = jax.nn.softplus(krdt + dtb_ref[...])
    qm = qm_ref[...]
    for h in range(MOBA_HEADS):
        qmo_ref[h] = _rope128(qm[:, h * MOBA_HEAD_DIM:(h + 1) * MOBA_HEAD_DIM], cosb, sinb)
    kmo_ref[...] = _rope128(km_ref[...], cosb, sinb)


def _prep(proj, tabs, tab_tiles, g_q, w_uq_p, g_kv, w_uk_t, dt_bias_p, tm):
    rows = proj.shape[0]

    def col(width, start):
        return pl.BlockSpec((tm, width), lambda i: (i, start // width))

    tab_spec = pl.BlockSpec((tm, 128), lambda i: (i % tab_tiles, 0))

    def full(a):
        return pl.BlockSpec(a.shape, lambda i: (0,) * a.ndim)

    heads_spec = lambda w: pl.BlockSpec((MLA_HEADS, tm, w), lambda i: (0, i, 0))
    row_spec = lambda w: pl.BlockSpec((tm, w), lambda i: (i, 0))
    return pl.pallas_call(
        _prep_kernel,
        grid=(rows // tm,),
        in_specs=[col(512, COL_CQ), col(512, COL_QM), col(256, COL_CKV), col(128, COL_KM), col(128, COL_KRDT),
                  tab_spec, tab_spec, tab_spec, tab_spec,
                  full(g_q), full(w_uq_p), full(g_kv), full(w_uk_t), full(dt_bias_p)],
        out_specs=[heads_spec(MLA_KV_RANK), heads_spec(MLA_ROPE_DIM), row_spec(MLA_KV_RANK), row_spec(MLA_ROPE_DIM),
                   heads_spec(MOBA_HEAD_DIM), row_spec(MOBA_HEAD_DIM), row_spec(128)],
        out_shape=[jax.ShapeDtypeStruct((MLA_HEADS, rows, MLA_KV_RANK), F32),
                   jax.ShapeDtypeStruct((MLA_HEADS, rows, MLA_ROPE_DIM), F32),
                   jax.ShapeDtypeStruct((rows, MLA_KV_RANK), F32),
                   jax.ShapeDtypeStruct((rows, MLA_ROPE_DIM), F32),
                   jax.ShapeDtypeStruct((MOBA_HEADS, rows, MOBA_HEAD_DIM), F32),
                   jax.ShapeDtypeStruct((rows, MOBA_HEAD_DIM), F32),
                   jax.ShapeDtypeStruct((rows, 128), F32)],
        compiler_params=_params("parallel"),
    )(proj, proj, proj, proj, proj, *tabs, g_q, w_uq_p, g_kv, w_uk_t, dt_bias_p)


def _rope_tables(pos):
    def tab(half):
        inv = ROPE_THETA ** (-jnp.arange(half, dtype=F32) / half)
        ang = pos.astype(F32)[:, None] * inv[None, :]
        return jnp.cos(ang), jnp.sin(ang)

    c32, s32 = tab(MLA_ROPE_DIM // 2)
    c64, s64 = tab(MOBA_HEAD_DIM // 2)
    cosa = jnp.concatenate([c32, c32, c32, c32], axis=1)
    sina = jnp.concatenate([-s32, s32, -s32, s32], axis=1)
    cosb = jnp.concatenate([c64, c64], axis=1)
    sinb = jnp.concatenate([-s64, s64], axis=1)
    return cosa, sina, cosb, sinb


def _conv_kernel(cur_ref, prev_ref, init_ref, w_ref, b_ref, o_ref, pad_scr, *, tiles_per_seq, tm):
    i = pl.program_id(0)
    at_start = (i % tiles_per_seq) == 0
    pad_scr[0:8, :] = jnp.where(at_start, init_ref[...], prev_ref[...])
    pad_scr[8:8 + tm, :] = cur_ref[...]
    acc = b_ref[...] + w_ref[CONV_WIDTH - 1:CONV_WIDTH, :] * cur_ref[...]
    for k in range(CONV_WIDTH - 1):
        shift = CONV_WIDTH - 1 - k
        acc = acc + w_ref[k:k + 1, :] * pad_scr[8 - shift:8 - shift + tm, :]
    o_ref[...] = _silu(acc)


def _conv(proj, init8, conv_w, conv_b, seq_len, tm):
    rows = proj.shape[0]
    tiles_per_seq = seq_len // tm
    blk = COL_XBC // CONV_DIM
    return pl.pallas_call(
        functools.partial(_conv_kernel, tiles_per_seq=tiles_per_seq, tm=tm),
        grid=(rows // tm,),
        in_specs=[pl.BlockSpec((tm, CONV_DIM), lambda i: (i, blk)),
                  pl.BlockSpec((8, CONV_DIM), lambda i: (jnp.maximum(i * (tm // 8) - 1, 0), blk)),
                  pl.BlockSpec((None, 8, CONV_DIM), lambda i: (i // tiles_per_seq, 0, 0)),
                  pl.BlockSpec((CONV_WIDTH, CONV_DIM), lambda i: (0, 0)),
                  pl.BlockSpec((1, CONV_DIM), lambda i: (0, 0))],
        out_specs=pl.BlockSpec((tm, CONV_DIM), lambda i: (i, 0)),
        out_shape=jax.ShapeDtypeStruct((rows, CONV_DIM), F32),
        scratch_shapes=[pltpu.VMEM((tm + 8, CONV_DIM), F32)],
        compiler_params=_params("arbitrary"),
    )(proj, proj, init8, conv_w, conv_b)


def _ssd_kernel(xc_ref, dtx_ref, z_ref, init_ref, alog_ref, dskip_ref, gssm_ref, y_ref, st_ref, *, lc):
    c = pl.program_id(1)

    @pl.when(c == 0)
    def _():
        st_ref[...] = init_ref[...]

    gw = SSM_D_INNER // SSM_GROUPS
    xs = xc_ref[:, 0:SSM_D_INNER]
    dt = dtx_ref[:, DT_LANE:DT_LANE + SSM_HEADS]
    a_head = -jnp.exp(alog_ref[...])
    adt = dt * a_head
    row = lax.broadcasted_iota(jnp.int32, (lc, lc), 0)
    colm = lax.broadcasted_iota(jnp.int32, (lc, lc), 1)
    tril = row >= colm
    a_cs = jnp.dot(tril.astype(F32), adt, preferred_element_type=F32, precision=HIGHEST)
    eye = (lax.broadcasted_iota(jnp.int32, (SSM_HEADS, SSM_HEADS), 0)
           == lax.broadcasted_iota(jnp.int32, (SSM_HEADS, SSM_HEADS), 1)).astype(F32)
    a_cs_t = _dot_nt(eye, a_cs, precision=HIGHEST)
    a_tot = a_cs[lc - 1:lc, :]
    spread = (lax.broadcasted_iota(jnp.int32, (SSM_HEADS, SSM_D_INNER), 1) // SSM_HEAD_DIM
              == lax.broadcasted_iota(jnp.int32, (SSM_HEADS, SSM_D_INNER), 0)).astype(F32).astype(BF16)
    spread_t = (lax.broadcasted_iota(jnp.int32, (SSM_D_INNER, SSM_HEADS), 0) // SSM_HEAD_DIM
                == lax.broadcasted_iota(jnp.int32, (SSM_D_INNER, SSM_HEADS), 1)).astype(F32).astype(BF16)

    def split3(v):
        hi = v.astype(BF16)
        rem = v - hi.astype(F32)
        mid = rem.astype(BF16)
        return hi, mid, (rem - mid.astype(F32)).astype(BF16)

    def expand(v):
        hi, mid, lo = split3(v)
        return _dot(hi, spread) + _dot(mid, spread) + _dot(lo, spread)

    xdt = xs * expand(dt)
    xd_state = (xdt * expand(jnp.exp(a_tot - a_cs))).astype(BF16)
    lane = lax.broadcasted_iota(jnp.int32, (lc, 128), 1)
    y_groups = []
    ds_groups = []
    for g in range(SSM_GROUPS):
        bg = xc_ref[:, SSM_D_INNER + g * SSM_STATE: SSM_D_INNER + (g + 1) * SSM_STATE].astype(BF16)
        cg = xc_ref[:, SSM_D_INNER + (SSM_GROUPS + g) * SSM_STATE:
                    SSM_D_INNER + (SSM_GROUPS + g + 1) * SSM_STATE].astype(BF16)
        cb = _dot_nt(cg, bg)
        s_g = st_ref[g * gw:(g + 1) * gw, :]
        y_off = _dot_nt(cg, s_g.astype(BF16))
        pairs = []
        for pr in range(gw // 128):
            x_pair = xdt[:, g * gw + pr * 128: g * gw + (pr + 1) * 128]
            acc = None
            for sub in range(2):
                h = (g * gw + pr * 128) // SSM_HEAD_DIM + sub
                seg = a_cs[:, h:h + 1] - a_cs_t[h:h + 1, :]
                decay = jnp.exp(jnp.where(tril, seg, -jnp.inf))
                gmat = (cb * decay).astype(BF16)
                x_h = jnp.where((lane // SSM_HEAD_DIM) == sub, x_pair, 0.0).astype(BF16)
                part = _dot(gmat, x_h)
                acc = part if acc is None else acc + part
            pairs.append(acc)
        y_groups.append((jnp.concatenate(pairs, axis=1), y_off))
        ds_groups.append(lax.dot_general(xd_state[:, g * gw:(g + 1) * gw], bg, TN_DIMS,
                                         preferred_element_type=F32))
    y_diag = jnp.concatenate([p[0] for p in y_groups], axis=1)
    y_off = jnp.concatenate([p[1] for p in y_groups], axis=1)
    y = y_diag + y_off * expand(jnp.exp(a_cs)) + expand(dskip_ref[...]) * xs
    tot_col = jnp.exp(a_cs_t[:, lc - 1:lc])
    t_hi, t_mid, t_lo = split3(jnp.broadcast_to(tot_col, (SSM_HEADS, SSM_STATE)))
    decay_full = _dot(spread_t, t_hi) + _dot(spread_t, t_mid) + _dot(spread_t, t_lo)
    st_ref[...] = decay_full * st_ref[...] + jnp.concatenate(ds_groups, axis=0)
    z = z_ref[...]
    v = y * _silu(z)
    outs = []
    for g in range(SSM_GROUPS):
        vg = v[:, g * gw:(g + 1) * gw]
        outs.append(vg * lax.rsqrt(jnp.mean(vg * vg, axis=-1, keepdims=True) + RMS_EPS))
    y_ref[...] = jnp.concatenate(outs, axis=1) * gssm_ref[...]


def _ssd(xconv, dtx, proj, init_state, layer, a_log, d_skip, g_ssm, seq_len):
    rows = xconv.shape[0]
    n_seq = rows // seq_len
    lc = min(SSM_CHUNK, seq_len)
    nc = seq_len // lc
    zblk = COL_Z // SSM_D_INNER
    vec = lambda a: pl.BlockSpec(a.shape, lambda b, c: (0, 0))
    return pl.pallas_call(
        functools.partial(_ssd_kernel, lc=lc),
        grid=(n_seq, nc),
        in_specs=[pl.BlockSpec((lc, CONV_DIM), lambda b, c: (b * nc + c, 0)),
                  pl.BlockSpec((lc, 128), lambda b, c: (b * nc + c, 0)),
                  pl.BlockSpec((lc, SSM_D_INNER), lambda b, c: (b * nc + c, zblk)),
                  pl.BlockSpec((None, None, SSM_D_INNER, SSM_STATE), lambda b, c: (layer, b, 0, 0)),
                  vec(a_log), vec(d_skip), vec(g_ssm)],
        out_specs=[pl.BlockSpec((lc, SSM_D_INNER), lambda b, c: (b * nc + c, 0)),
                   pl.BlockSpec((None, SSM_D_INNER, SSM_STATE), lambda b, c: (b, 0, 0))],
        out_shape=[jax.ShapeDtypeStruct((rows, SSM_D_INNER), F32),
                   jax.ShapeDtypeStruct((n_seq, SSM_D_INNER, SSM_STATE), F32)],
        compiler_params=_params("parallel", "arbitrary"),
    )(xconv, dtx, proj, init_state, a_log, d_skip, g_ssm)


def _softmax_init(m_scr, l_scr, acc_scr):
    m_scr[...] = jnp.full_like(m_scr, MASK_VALUE)
    l_scr[...] = jnp.zeros_like(l_scr)
    acc_scr[...] = jnp.zeros_like(acc_scr)


def _softmax_step(s, m_scr, l_scr, acc_scr, pv):
    m_prev = m_scr[...]
    m_new = jnp.maximum(m_prev, jnp.max(s, axis=1, keepdims=True))
    alpha = jnp.exp(m_prev - m_new)
    p = jnp.exp(s - m_new)
    l_scr[...] = alpha * l_scr[...] + jnp.sum(p, axis=1, keepdims=True)
    acc_scr[...] = alpha * acc_scr[...] + pv(p.astype(BF16))
    m_scr[...] = m_new


def _transpose_mxu(x):
    c = x.shape[1]
    eye = (lax.broadcasted_iota(jnp.int32, (c, c), 0) == lax.broadcasted_iota(jnp.int32, (c, c), 1))
    return _dot_nt(jnp.where(eye, 1.0, 0.0).astype(BF16), x).astype(BF16)


def _pad_rows(x, n):
    return jnp.concatenate([x, jnp.zeros((n - x.shape[0], x.shape[1]), x.dtype)], axis=0)


KV_UNROLL = 1
def _mla_prompt_kernel(ql_ref, qr_ref, ckv_ref, kr_ref, wuv_ref, o_ref,
                       ckv_b, kr_b, m_scr, l_scr, acc_scr, *, tq, tk):
    i = pl.program_id(1)

    @pl.when(i == 0)
    def _():
        ckv_b[...] = ckv_ref[...].astype(BF16)
        kr_b[...] = kr_ref[...].astype(BF16)

    rows = MLA_HEADS * tq
    q1 = ql_ref[...].reshape(rows, MLA_KV_RANK).astype(BF16)
    q2 = qr_ref[...].reshape(rows, MLA_ROPE_DIM).astype(BF16)
    _softmax_init(m_scr, l_scr, acc_scr)
    q_pos = i * tq + lax.broadcasted_iota(jnp.int32, (rows, tk), 0) % tq
    key_off = lax.broadcasted_iota(jnp.int32, (rows, tk), 1)

    def body(j, carry):
        for u in range(KV_UNROLL):
            ks = pl.multiple_of((j * KV_UNROLL + u) * tk, tk)
            kc = ckv_b[pl.ds(ks, tk), :]
            s = (_dot_nt(q1, kc) + _dot_nt(q2, kr_b[pl.ds(ks, tk), :])) * MLA_SCALE
            s = jnp.where(key_off + ks <= q_pos, s, MASK_VALUE)
            _softmax_step(s, m_scr, l_scr, acc_scr, lambda p: _dot(p, kc))
        return carry

    lax.fori_loop(0, (i * tq) // (tk * KV_UNROLL) + 1, body, 0)
    lat = (acc_scr[...] / l_scr[...]).astype(BF16)
    for h in range(MLA_HEADS):
        o_ref[:, h * MLA_V_DIM:(h + 1) * MLA_V_DIM] = _dot(lat[h * tq:(h + 1) * tq], wuv_ref[h])


def _mla_prompt(q_lat, q_rope, ckv, krope, w_uv_b, n_seq, seq_len):
    tq = 256
    tk = 1024
    assert seq_len % (tk * KV_UNROLL) == 0 and (tk * KV_UNROLL) % tq == 0
    nt = seq_len // tq
    rows = n_seq * seq_len
    return pl.pallas_call(
        functools.partial(_mla_prompt_kernel, tq=tq, tk=tk),
        grid=(n_seq, nt),
        in_specs=[pl.BlockSpec((MLA_HEADS, tq, MLA_KV_RANK), lambda b, i: (0, b * nt + i, 0)),
                  pl.BlockSpec((MLA_HEADS, tq, MLA_ROPE_DIM), lambda b, i: (0, b * nt + i, 0)),
                  pl.BlockSpec((seq_len, MLA_KV_RANK), lambda b, i: (b, 0)),
                  pl.BlockSpec((seq_len, MLA_ROPE_DIM), lambda b, i: (b, 0)),
                  pl.BlockSpec(w_uv_b.shape, lambda b, i: (0, 0, 0))],
        out_specs=pl.BlockSpec((tq, MLA_HEADS * MLA_V_DIM), lambda b, i: (b * nt + i, 0)),
        out_shape=jax.ShapeDtypeStruct((rows, MLA_HEADS * MLA_V_DIM), F32),
        scratch_shapes=[pltpu.VMEM((seq_len, MLA_KV_RANK), BF16), pltpu.VMEM((seq_len, MLA_ROPE_DIM), BF16),
                        pltpu.VMEM((MLA_HEADS * tq, 1), F32), pltpu.VMEM((MLA_HEADS * tq, 1), F32),
                        pltpu.VMEM((MLA_HEADS * tq, MLA_KV_RANK), F32)],
        compiler_params=_params("parallel", "arbitrary"),
    )(q_lat, q_rope, ckv, krope, w_uv_b)


def _moba_select(gate, n_valid, block_axis):
    nb = gate.shape[block_axis]
    blk = lax.broadcasted_iota(jnp.int32, gate.shape, block_axis)
    valid = blk < n_valid
    g = jnp.where(valid, gate, -jnp.inf)
    rank = jnp.zeros(gate.shape, jnp.int32)
    for m in range(nb):
        gm = lax.slice_in_dim(g, m, m + 1, axis=block_axis)
        ahead = (gm > g) | ((gm == g) & (m < blk))
        rank = rank + jnp.where(ahead, 1, 0)
    return jnp.where(valid & (rank < MOBA_TOPK), 1.0, 0.0).astype(F32)


def _moba_prompt_kernel(q_ref, k_ref, v_ref, o_ref, k_b, v_b, kmean_scr, m_scr, l_scr, acc_scr, *, nb):
    i = pl.program_id(1)
    tq = MOBA_BLOCK
    rows = MOBA_HEADS * tq

    @pl.when(i == 0)
    def _():
        k = k_ref[...]
        k_b[...] = k.astype(BF16)
        v_b[...] = v_ref[...].astype(BF16)
        kmean_scr[...] = jnp.zeros_like(kmean_scr)
        kmean_scr[0:nb, :] = jnp.mean(k.reshape(nb, MOBA_BLOCK, MOBA_HEAD_DIM), axis=1)

    q = q_ref[...].reshape(rows, MOBA_HEAD_DIM)
    qb = q.astype(BF16)
    gate_t = _dot_nt(kmean_scr[...], q, precision=HIGHEST)
    sel_t = _moba_select(gate_t[0:nb], i, 0)
    sel = jnp.concatenate([sel_t, jnp.zeros((128 - nb, rows), F32)], axis=0).T
    blk_lane = lax.broadcasted_iota(jnp.int32, (rows, 128), 1)
    _softmax_init(m_scr, l_scr, acc_scr)

    def attend(j, n_blocks, mask_fn):
        ks = pl.multiple_of(j * MOBA_BLOCK, MOBA_BLOCK)
        s = _dot_nt(qb, k_b[pl.ds(ks, n_blocks * MOBA_BLOCK), :]) * MOBA_SCALE
        s = jnp.where(mask_fn(), s, MASK_VALUE)
        _softmax_step(s, m_scr, l_scr, acc_scr, lambda p: _dot(p, v_b[pl.ds(ks, n_blocks * MOBA_BLOCK), :]))

    t_q = lax.broadcasted_iota(jnp.int32, (rows, MOBA_BLOCK), 0) % tq
    t_k = lax.broadcasted_iota(jnp.int32, (rows, MOBA_BLOCK), 1)
    attend(i, 1, lambda: t_k <= t_q)

    def picked(j):
        col = jnp.sum(jnp.where(blk_lane == j, sel, 0.0), axis=1, keepdims=True) > 0.5
        return jnp.broadcast_to(col, (rows, MOBA_BLOCK))

    def body(j, carry):
        attend(2 * j, 2, lambda: jnp.concatenate([picked(2 * j), picked(2 * j + 1)], axis=1))
        return carry

    lax.fori_loop(0, (i + 1) // 2, body, 0)
    out = acc_scr[...] / l_scr[...]
    for h in range(MOBA_HEADS):
        o_ref[:, h * MOBA_HEAD_DIM:(h + 1) * MOBA_HEAD_DIM] = out[h * tq:(h + 1) * tq]


def _moba_prompt(qm, km, proj, n_seq, seq_len):
    nb = seq_len // MOBA_BLOCK
    rows = n_seq * seq_len
    vblk = COL_VM // MOBA_HEAD_DIM
    return pl.pallas_call(
        functools.partial(_moba_prompt_kernel, nb=nb),
        grid=(n_seq, nb),
        in_specs=[pl.BlockSpec((MOBA_HEADS, MOBA_BLOCK, MOBA_HEAD_DIM), lambda b, i: (0, b * nb + i, 0)),
                  pl.BlockSpec((seq_len, MOBA_HEAD_DIM), lambda b, i: (b, 0)),
                  pl.BlockSpec((seq_len, MOBA_HEAD_DIM), lambda b, i: (b, vblk))],
        out_specs=pl.BlockSpec((MOBA_BLOCK, MOBA_HEADS * MOBA_HEAD_DIM), lambda b, i: (b * nb + i, 0)),
        out_shape=jax.ShapeDtypeStruct((rows, MOBA_HEADS * MOBA_HEAD_DIM), F32),
        scratch_shapes=[pltpu.VMEM((seq_len, MOBA_HEAD_DIM), BF16), pltpu.VMEM((seq_len, MOBA_HEAD_DIM), BF16),
                        pltpu.VMEM((128, MOBA_HEAD_DIM), F32),
                        pltpu.VMEM((MOBA_HEADS * MOBA_BLOCK, 1), F32), pltpu.VMEM((MOBA_HEADS * MOBA_BLOCK, 1), F32),
                        pltpu.VMEM((MOBA_HEADS * MOBA_BLOCK, MOBA_HEAD_DIM), F32)],
        compiler_params=_params("parallel", "arbitrary"),
    )(qm, km, proj)


SAMPLE_CHUNKS = 2
SCORE_SPLIT = 2


def _attn_sample_kernel(pt_ref, ckv_pool, kr_pool, mk_pool, mv_pool,
                        ql_ref, qr_ref, qm_ref, cnew_ref, knew_ref, kmnew_ref, vmnew_ref, wuv_ref,
                        mla_o_ref, moba_o_ref,
                        ckv_buf, kr_buf, mk_buf, mv_buf, kstash, kmean_scr, selmask,
                        m1, l1, acc1, m2, l2, acc2, sems, *, layer, t_new, ch):
    seq = pl.program_id(0)
    n_seq = pl.num_programs(0)
    rows = MLA_HEADS * t_new
    ckeys = ch * PAGE_SIZE
    nb_chunk = ckeys // MOBA_BLOCK

    def copies_a(page_of, slot):
        out = []
        for j in range(ch):
            page, win = page_of(j), pl.ds(j * PAGE_SIZE, PAGE_SIZE)
            out.append(pltpu.make_async_copy(ckv_pool.at[layer, page], ckv_buf.at[slot, win, :], sems.at[0, slot]))
            out.append(pltpu.make_async_copy(kr_pool.at[layer, page], kr_buf.at[slot, :, win], sems.at[1, slot]))
            out.append(pltpu.make_async_copy(mk_pool.at[layer, page], mk_buf.at[slot, win, :], sems.at[2, slot]))
        return out

    def copies_b(page_of, slot):
        return [pltpu.make_async_copy(mv_pool.at[layer, page_of(j)],
                                      mv_buf.at[slot, pl.ds(j * PAGE_SIZE, PAGE_SIZE), :], sems.at[3, slot])
                for j in range(ch)]

    def start(copies, sq, k):
        for c in copies(lambda j: pt_ref[sq, k * ch + j], k % 2):
            c.start()

    def wait(copies, k):
        for c in copies(lambda j: 0, k % 2):
            c.wait()

    def refill(copies, k):
        if k + 2 < SAMPLE_CHUNKS:
            start(copies, seq, k + 2)
        else:
            @pl.when(seq + 1 < n_seq)
            def _():
                start(copies, seq + 1, k + 2 - SAMPLE_CHUNKS)

    @pl.when(seq == 0)
    def _():
        for k in range(2):
            start(copies_a, seq, k)
            start(copies_b, seq, k)

    t_q = lax.broadcasted_iota(jnp.int32, (rows, t_new), 0) % t_new
    t_k = lax.broadcasted_iota(jnp.int32, (rows, t_new), 1)
    causal_new = t_k <= t_q

    def scores(q_rows, keys):
        n = keys.shape[0] // SCORE_SPLIT
        return jnp.concatenate([_dot_nt(q_rows, keys[u * n:(u + 1) * n]) for u in range(SCORE_SPLIT)], axis=1)

    q1 = ql_ref[...].reshape(rows, MLA_KV_RANK).astype(BF16)
    q2 = qr_ref[...].reshape(rows, MLA_ROPE_DIM).astype(BF16)
    _softmax_init(m1, l1, acc1)
    for k in range(SAMPLE_CHUNKS):
        slot = k % 2
        wait(copies_a, k)
        kc = ckv_buf[slot].astype(BF16)
        s = (scores(q1, kc) + _dot(q2, kr_buf[slot].astype(BF16))) * MLA_SCALE
        _softmax_step(s, m1, l1, acc1, lambda p: _dot(p, kc))
        mk = mk_buf[slot]
        kstash[k * ckeys:(k + 1) * ckeys, :] = mk.astype(BF16)
        kmean_scr[k * nb_chunk:(k + 1) * nb_chunk, :] = jnp.mean(
            mk.reshape(nb_chunk, MOBA_BLOCK, MOBA_HEAD_DIM), axis=1)
        refill(copies_a, k)

    cn = cnew_ref[...].astype(BF16)
    sn = (_dot_nt(q1, cn) + _dot_nt(q2, knew_ref[...].astype(BF16))) * MLA_SCALE
    _softmax_step(jnp.where(causal_new, sn, MASK_VALUE), m1, l1, acc1, lambda p: _dot(p, cn))
    lat = (acc1[...] / l1[...]).astype(BF16)
    for h in range(MLA_HEADS):
        mla_o_ref[:, h * MLA_V_DIM:(h + 1) * MLA_V_DIM] = _dot(lat[h * t_new:(h + 1) * t_new], wuv_ref[h])

    q = qm_ref[...].reshape(rows, MOBA_HEAD_DIM)
    qb = q.astype(BF16)
    _softmax_init(m2, l2, acc2)
    vn = vmnew_ref[...].astype(BF16)
    sm = _dot_nt(qb, kmnew_ref[...].astype(BF16)) * MOBA_SCALE
    _softmax_step(jnp.where(causal_new, sm, MASK_VALUE), m2, l2, acc2, lambda p: _dot(p, vn))
    nb_past = SAMPLE_CHUNKS * nb_chunk
    gate = _dot_nt(q, kmean_scr[...], precision=HIGHEST)
    sel = _moba_select(gate, nb_past, 1)
    for b in range(nb_past):
        selmask[:, b * MOBA_BLOCK:(b + 1) * MOBA_BLOCK] = jnp.broadcast_to(sel[:, b:b + 1], (rows, MOBA_BLOCK))

    for k in range(SAMPLE_CHUNKS):
        slot = k % 2
        wait(copies_b, k)
        v = mv_buf[slot].astype(BF16)
        s = scores(qb, kstash[k * ckeys:(k + 1) * ckeys, :]) * MOBA_SCALE
        s = jnp.where(selmask[:, k * ckeys:(k + 1) * ckeys] > 0.5, s, MASK_VALUE)
        _softmax_step(s, m2, l2, acc2, lambda p: _dot(p, v))
        refill(copies_b, k)

    out = acc2[...] / l2[...]
    for h in range(MOBA_HEADS):
        moba_o_ref[:, h * MOBA_HEAD_DIM:(h + 1) * MOBA_HEAD_DIM] = out[h * t_new:(h + 1) * t_new]


def _attn_sample(page_table, ckv_pool, kr_pool_t, mk_pool, mv_pool, layer,
                 q_lat, q_rope, qm, ckv_new, kr_new, km_new, proj, w_uv_b, t_new):
    n_seq, n_pages = page_table.shape
    assert n_pages % (SAMPLE_CHUNKS * MOBA_BLOCK // PAGE_SIZE) == 0
    ch = n_pages // SAMPLE_CHUNKS
    ckeys = ch * PAGE_SIZE
    past_len = n_pages * PAGE_SIZE
    rows = MLA_HEADS * t_new
    vblk = COL_VM // MOBA_HEAD_DIM
    hbm = pl.BlockSpec(memory_space=pl.ANY)
    heads = lambda w: pl.BlockSpec((MLA_HEADS, t_new, w), lambda s, pt: (0, s, 0))
    new_rows = lambda w, blk=0: pl.BlockSpec((t_new, w), lambda s, pt: (s, blk))
    grid_spec = pltpu.PrefetchScalarGridSpec(
        num_scalar_prefetch=1, grid=(n_seq,),
        in_specs=[hbm, hbm, hbm, hbm,
                  heads(MLA_KV_RANK), heads(MLA_ROPE_DIM), heads(MOBA_HEAD_DIM),
                  new_rows(MLA_KV_RANK), new_rows(MLA_ROPE_DIM), new_rows(MOBA_HEAD_DIM),
                  new_rows(MOBA_HEAD_DIM, vblk),
                  pl.BlockSpec(w_uv_b.shape, lambda s, pt: (0, 0, 0))],
        out_specs=[new_rows(MLA_HEADS * MLA_V_DIM), new_rows(MOBA_HEADS * MOBA_HEAD_DIM)],
        scratch_shapes=[pltpu.VMEM((2, ckeys, MLA_KV_RANK), F32),
                        pltpu.VMEM((2, MLA_ROPE_DIM, ckeys), F32),
                        pltpu.VMEM((2, ckeys, MOBA_HEAD_DIM), F32),
                        pltpu.VMEM((2, ckeys, MOBA_HEAD_DIM), F32),
                        pltpu.VMEM((past_len, MOBA_HEAD_DIM), BF16),
                        pltpu.VMEM((past_len // MOBA_BLOCK, MOBA_HEAD_DIM), F32),
                        pltpu.VMEM((rows, past_len), F32),
                        pltpu.VMEM((rows, 1), F32), pltpu.VMEM((rows, 1), F32),
                        pltpu.VMEM((rows, MLA_KV_RANK), F32),
                        pltpu.VMEM((rows, 1), F32), pltpu.VMEM((rows, 1), F32),
                        pltpu.VMEM((rows, MOBA_HEAD_DIM), F32),
                        pltpu.SemaphoreType.DMA((4, 2))])
    return pl.pallas_call(
        functools.partial(_attn_sample_kernel, layer=layer, t_new=t_new, ch=ch),
        grid_spec=grid_spec,
        out_shape=[jax.ShapeDtypeStruct((n_seq * t_new, MLA_HEADS * MLA_V_DIM), F32),
                   jax.ShapeDtypeStruct((n_seq * t_new, MOBA_HEADS * MOBA_HEAD_DIM), F32)],
        compiler_params=_params("arbitrary"),
    )(page_table, ckv_pool, kr_pool_t, mk_pool, mv_pool, q_lat, q_rope, qm, ckv_new, kr_new, km_new, proj, w_uv_b)


def _outproj_kernel(mla_ref, moba_ref, ssm_ref, w_ref, x_ref, gt_ref, lg_ref, lb_ref, o_ref, *, alpha):
    w_mla = MLA_HEADS * MLA_V_DIM
    w_moba = MOBA_HEADS * MOBA_HEAD_DIM
    mixed = (_dot(mla_ref[...].astype(BF16), w_ref[0:w_mla, :])
             + _dot(moba_ref[...].astype(BF16), w_ref[w_mla:w_mla + w_moba, :])
             + _dot(ssm_ref[...].astype(BF16), w_ref[w_mla + w_moba:, :]))
    v = alpha * x_ref[...] + _mod_rows(gt_ref, x_ref.shape[0]) * mixed
    o_ref[...] = _layer_norm_rows(v, lg_ref[...], lb_ref[...])


def _outproj(mla_out, moba_out, ssm_out, w_out_b, layer, x, mod, ln_g, ln_b, alpha, tm):
    rows = x.shape[0]
    row = lambda w: pl.BlockSpec((tm, w), lambda i: (i, 0))
    vec_spec = pl.BlockSpec((1, D_MODEL), lambda i: (0, 0))
    return pl.pallas_call(
        functools.partial(_outproj_kernel, alpha=alpha),
        grid=(rows // tm,),
        in_specs=[row(mla_out.shape[1]), row(moba_out.shape[1]), row(ssm_out.shape[1]),
                  pl.BlockSpec((None, D_MODEL, D_MODEL), lambda i: (layer, 0, 0)),
                  row(D_MODEL), mod.spec(tm), vec_spec, vec_spec],
        out_specs=row(D_MODEL),
        out_shape=jax.ShapeDtypeStruct((rows, D_MODEL), F32),
        compiler_params=_params("parallel"),
    )(mla_out, moba_out, ssm_out, w_out_b, x, mod.arrays[2], ln_g, ln_b)


def _trunk(x, mods_by_layer, seq_len, tabs, conv_init, ssm_init, attn, w, tm):
    depth = w["w_in_p"].shape[0]
    alpha = (2 * depth) ** 0.25
    n_seq = x.shape[0] // seq_len
    tab_tiles = tabs[0].shape[0] // tm
    states = []
    for l in range(depth):
        mods = mods_by_layer[l]
        ln_g, ln_b = w["ln_g"][l], w["ln_b"][l]
        x = _ffn(x, mods[0], w["wg"], w["wu"], w["wd"], ln_g[0:1], ln_b[0:1], l, 0, alpha, tm)
        proj = _inproj(x, mods[1], w["w_in_p"], l, tm)
        q_lat, q_rope, ckv, krope, qm, km, dtx = _prep(
            proj, tabs, tab_tiles, w["g_q"][l:l + 1], w["w_uq_p"][l], w["g_kv"][l:l + 1], w["w_uk_t"][l],
            w["dt_bias_p"][l:l + 1], tm)
        conv_tm = min(tm, seq_len)
        init8 = jnp.pad(conv_init[l], ((0, 0), (8 - (CONV_WIDTH - 1), 0), (0, 0)))
        xconv = _conv(proj, init8, w["conv_w"][l], w["conv_b"][l:l + 1], seq_len, conv_tm)
        ssm_out, ssm_state = _ssd(xconv, dtx, proj, ssm_init, l, w["a_log"][l:l + 1], w["d_skip"][l:l + 1],
                                  w["g_ssm"][l:l + 1], seq_len)
        mla_out, moba_out = attn(l, q_lat, q_rope, ckv, krope, qm, km, proj)
        x = _outproj(mla_out, moba_out, ssm_out, w["w_out_b"], l, x, mods[1], ln_g[1:2], ln_b[1:2], alpha, tm)
        x = _ffn(x, mods[2], w["wg"], w["wu"], w["wd"], ln_g[2:3], ln_b[2:3], l, 1, alpha, tm)
        vm = proj[:, COL_VM:COL_VM + MOBA_HEAD_DIM]
        keep = CONV_WIDTH - 1
        xbc_tail = proj.reshape(n_seq, seq_len, IN_COLS_PADDED)[:, seq_len - min(keep, seq_len):,
                                                                COL_XBC:COL_XBC + CONV_DIM]
        conv_state = jnp.concatenate([conv_init[l], xbc_tail], axis=1)[:, -keep:]
        states.append((ckv.reshape(n_seq, seq_len, MLA_KV_RANK),
                       krope.reshape(n_seq, seq_len, MLA_ROPE_DIM),
                       km.reshape(n_seq, seq_len, 1, MOBA_HEAD_DIM),
                       vm.reshape(n_seq, seq_len, 1, MOBA_HEAD_DIM),
                       ssm_state.reshape(n_seq, SSM_HEADS, SSM_HEAD_DIM, SSM_STATE),
                       conv_state))
    return x, [jnp.stack([s[i] for s in states]) for i in range(len(states[0]))]


def _row_tile(rows, seq_len, largest):
    for tm in (512, 256, 128, 64, 32, 16, 8):
        if tm <= largest and rows % tm == 0 and (seq_len % tm == 0 or tm % seq_len == 0):
            return tm
    raise ValueError("row count must be a multiple of 8")


def kernel(x_prompt, x_sample, cache_mla_ckv, cache_mla_krope, cache_moba_k, cache_moba_v, state_ssm, state_conv,
           page_table, c_prompt, c_sample, w_ada, b_ada, ln_g, ln_b, ffn_w_gate, ffn_w_up, ffn_w_down,
           w_in, g_q, w_uq, g_kv, w_uk, w_uv, conv_w, conv_b, dt_bias, a_log, d_skip, g_ssm, w_out):
    depth = w_in.shape[0]
    n_prompt, seq_len, _ = x_prompt.shape
    n_sample, t_new, _ = x_sample.shape
    n_pages = page_table.shape[1]
    past_len = n_pages * PAGE_SIZE
    n_pool = cache_mla_ckv.shape[1]

    assert past_len % MOBA_BLOCK == 0 and t_new <= MOBA_BLOCK and seq_len % MOBA_BLOCK == 0
    names = ["cq", "ckv", "kr", "qm", "km", "vm", "z", "xbc", "dt"]
    widths = [MLA_Q_RANK, MLA_KV_RANK, MLA_ROPE_DIM, MOBA_HEADS * MOBA_HEAD_DIM, MOBA_HEAD_DIM, MOBA_HEAD_DIM,
              SSM_D_INNER, CONV_DIM, SSM_HEADS]
    pieces, start = {}, 0
    for name, width in zip(names, widths):
        pieces[name] = (start, start + width)
        start += width
    order = ["cq", "qm", "ckv", "km", "vm", "xbc", "z", "kr", "dt"]
    w_in_p = jnp.concatenate([w_in[:, :, pieces[k][0]:pieces[k][1]] for k in order]
                             + [jnp.zeros((depth, D_MODEL, IN_COLS_PADDED - w_in.shape[2]), w_in.dtype)],
                             axis=2).astype(BF16)
    qk = MLA_NOPE_DIM + MLA_ROPE_DIM
    w_uq_h = w_uq.reshape(depth, MLA_Q_RANK, MLA_HEADS, qk)
    w_uq_p = jnp.concatenate([w_uq_h[..., :MLA_NOPE_DIM].reshape(depth, MLA_Q_RANK, MLA_HEADS * MLA_NOPE_DIM),
                              w_uq_h[..., MLA_NOPE_DIM:].reshape(depth, MLA_Q_RANK, MLA_HEADS * MLA_ROPE_DIM)],
                             axis=2).astype(BF16)
    dt_bias_p = jnp.zeros((depth, 128), F32).at[:, DT_LANE:DT_LANE + SSM_HEADS].set(dt_bias)
    w = dict(
        wg=ffn_w_gate.astype(BF16), wu=ffn_w_up.astype(BF16), wd=ffn_w_down.astype(BF16),
        w_in_p=w_in_p, w_uq_p=w_uq_p, w_uk_t=jnp.swapaxes(w_uk, 2, 3).astype(BF16), w_uv_b=w_uv.astype(BF16),
        w_out_b=w_out.astype(BF16), g_q=g_q, g_kv=g_kv, ln_g=ln_g, ln_b=ln_b,
        conv_w=conv_w.reshape(depth, CONV_WIDTH, CONV_DIM), conv_b=conv_b, dt_bias_p=dt_bias_p,
        a_log=a_log, d_skip=d_skip, g_ssm=g_ssm)

    n_c = n_prompt + n_sample
    c_rows = -(-n_c // 8) * 8
    c_all = jnp.concatenate([c_prompt, c_sample, jnp.zeros((c_rows - n_c, D_MODEL), F32)], axis=0)
    mod = _ada(c_all, w_ada, b_ada)
    mods_p = [_make_mods(mod[l, :n_prompt], seq_len) for l in range(depth)]
    mods_s = [_make_mods(mod[l, n_prompt:n_c], t_new) for l in range(depth)]

    tabs_p = _rope_tables(jnp.arange(seq_len, dtype=jnp.int32))
    conv0 = jnp.zeros((depth, n_prompt, CONV_WIDTH - 1, CONV_DIM), F32)
    ssm0 = jnp.zeros((depth, n_prompt, SSM_D_INNER, SSM_STATE), F32)

    def attn_prompt(l, q_lat, q_rope, ckv, krope, qm, km, proj):
        return (_mla_prompt(q_lat, q_rope, ckv, krope, w["w_uv_b"][l], n_prompt, seq_len),
                _moba_prompt(qm, km, proj, n_prompt, seq_len))

    tm_p = _row_tile(n_prompt * seq_len, seq_len, 512)
    y_p, st_p = _trunk(x_prompt.reshape(n_prompt * seq_len, D_MODEL), mods_p, seq_len, tabs_p, conv0, ssm0,
                       attn_prompt, w, tm_p)

    pos_s = past_len + jnp.arange(t_new, dtype=jnp.int32)
    tabs_s = tuple(jnp.tile(t, (n_sample, 1)) for t in _rope_tables(pos_s))
    mk_pool = cache_moba_k.reshape(depth, n_pool, PAGE_SIZE, MOBA_HEAD_DIM)
    mv_pool = cache_moba_v.reshape(depth, n_pool, PAGE_SIZE, MOBA_HEAD_DIM)

    kr_pool_t = jnp.swapaxes(cache_mla_krope, 2, 3)

    def attn_sample(l, q_lat, q_rope, ckv, krope, qm, km, proj):
        return _attn_sample(page_table, cache_mla_ckv, kr_pool_t, mk_pool, mv_pool, l,
                            q_lat, q_rope, qm, ckv, krope, km, proj, w["w_uv_b"][l], t_new)

    tm_s = _row_tile(n_sample * t_new, t_new, 512)
    ssm_init_s = state_ssm.reshape(depth, n_sample, SSM_D_INNER, SSM_STATE)
    y_s, st_s = _trunk(x_sample.reshape(n_sample * t_new, D_MODEL), mods_s, t_new, tabs_s, state_conv, ssm_init_s,
                       attn_sample, w, tm_s)

    return (y_p.reshape(n_prompt, seq_len, D_MODEL), y_s.reshape(n_sample, t_new, D_MODEL),
            *st_p, *st_s)
```
